```python
import jax, jax.numpy as jnp
from jax import lax
import numpy as np

D_MODEL = 1024
BATCH = 8
SEQ = 8192
DEPTH = 1

CHUNK = 64
N_LEFT_CHUNKS = 8
A_HEADS = 8
A_HEAD_DIM = 64
A_WIDTH = A_HEADS * A_HEAD_DIM
REL_CLIP = 128
R_HEADS = 8
R_KEY_DIM = 64
R_VAL_DIM = 128
R_QK_WIDTH = R_HEADS * R_KEY_DIM
R_V_WIDTH = R_HEADS * R_VAL_DIM
ROPE_BASE = 10000.0
NORM_EPS = 1e-6
GN_EPS = 1e-5
NEG_INF = -1e30
IN_SPLITS = (A_WIDTH, A_WIDTH, A_WIDTH, A_WIDTH, R_QK_WIDTH, R_QK_WIDTH, R_V_WIDTH, R_V_WIDTH, D_MODEL, D_MODEL)
IN_WIDTH = 4 * A_WIDTH + 2 * R_QK_WIDTH + 2 * R_V_WIDTH + 2 * D_MODEL

kernel_name = "hybrid_chunk_attn_retention_gated"


def rms_norm(x, gain):
    xf = x.astype(jnp.float32)
    y = xf * lax.rsqrt(jnp.mean(xf * xf, axis=-1, keepdims=True) + NORM_EPS)
    return (y * gain.astype(jnp.float32)).astype(x.dtype)


def split_columns(proj):
    outs = []
    start = 0
    for width in IN_SPLITS:
        outs.append(proj[..., start:start + width])
        start += width
    return outs


def rotary(x, positions):
    half = x.shape[-1] // 2
    inv_freq = jnp.power(ROPE_BASE, -jnp.arange(half, dtype=jnp.float32) / half)
    ang = positions.astype(jnp.float32)[:, None] * inv_freq[None, :]
    cos = jnp.cos(ang)[None, :, None, :]
    sin = jnp.sin(ang)[None, :, None, :]
    x1 = x[..., :half].astype(jnp.float32)
    x2 = x[..., half:].astype(jnp.float32)
    out = jnp.concatenate([x1 * cos - x2 * sin, x1 * sin + x2 * cos], axis=-1)
    return out.astype(x.dtype)


def chunk_band_attention(q, k, v, rel_bias):
    B, S, H, d = q.shape
    n_chunks = S // CHUNK
    band = (N_LEFT_CHUNKS + 1) * CHUNK
    q_local = jnp.arange(CHUNK)
    p = jnp.arange(band)
    dist = q_local[:, None] + N_LEFT_CHUNKS * CHUNK - p[None, :]
    idx = jnp.clip(dist, -REL_CLIP, REL_CLIP) + REL_CLIP
    bias = rel_bias[:, idx].astype(jnp.float32)
    valid = (jnp.arange(n_chunks)[:, None] - N_LEFT_CHUNKS + p[None, :] // CHUNK) >= 0
    scale = d ** -0.5

    def one_sequence(args):
        qb, kb, vb = args
        qc = qb.reshape(n_chunks, CHUNK, H, d)
        pad = jnp.zeros((N_LEFT_CHUNKS, CHUNK, H, d), kb.dtype)
        kp = jnp.concatenate([pad, kb.reshape(n_chunks, CHUNK, H, d)], axis=0)
        vp = jnp.concatenate([pad, vb.reshape(n_chunks, CHUNK, H, d)], axis=0)
        k_band = jnp.concatenate([kp[i:i + n_chunks] for i in range(N_LEFT_CHUNKS + 1)], axis=1)
        v_band = jnp.concatenate([vp[i:i + n_chunks] for i in range(N_LEFT_CHUNKS + 1)], axis=1)
        s = jnp.einsum('nqhd,nkhd->nhqk', qc, k_band).astype(jnp.float32) * scale + bias[None]
        s = jnp.where(valid[:, None, None, :], s, NEG_INF)
        w = jax.nn.softmax(s, axis=-1).astype(vb.dtype)
        o = jnp.einsum('nhqk,nkhd->nqhd', w, v_band)
        return o.reshape(S, H, d)

    return lax.map(one_sequence, (q, k, v))


def retention_chunkwise(q, k, v, gn_gain):
    B, S, H, dk = q.shape
    dv = v.shape[-1]
    n_chunks = S // CHUNK
    log_gamma = jnp.log1p(-jnp.exp2(-5.0 - jnp.arange(H, dtype=jnp.float32)))
    j = jnp.arange(CHUNK, dtype=jnp.float32)
    intra_decay = jnp.exp(jnp.abs(j[:, None] - j[None, :])[None] * log_gamma[:, None, None])
    cross_decay = jnp.exp((j + 1.0)[None, :] * log_gamma[:, None])
    state_decay = jnp.exp((CHUNK - 1.0 - j)[None, :] * log_gamma[:, None])
    chunk_decay = jnp.exp(CHUNK * log_gamma)

    qc = q.reshape(B, n_chunks, CHUNK, H, dk) * (dk ** -0.5)
    kc = k.reshape(B, n_chunks, CHUNK, H, dk)
    vc = v.reshape(B, n_chunks, CHUNK, H, dv)

    s = jnp.einsum('bnqhd,bnkhd->bnhqk', qc, kc) * intra_decay
    o_intra = jnp.einsum('bnhqk,bnkhe->bnqhe', s, vc)

    kv = jnp.einsum('bnkhd,hk,bnkhe->nbhde', kc, state_decay, vc).astype(jnp.float32)

    def step(state, kv_n):
        return state * chunk_decay[None, :, None, None] + kv_n, state

    _, state_prev = lax.scan(step, jnp.zeros((B, H, dk, dv), jnp.float32), kv)
    o_cross = jnp.einsum('bnqhd,nbhde->bnqhe', qc, state_prev) * cross_decay.T[:, :, None]

    o = (o_intra + o_cross).astype(jnp.float32).reshape(B, S, H, dv)
    mean = jnp.mean(o, axis=-1, keepdims=True)
    var = jnp.mean(jnp.square(o - mean), axis=-1, keepdims=True)
    o = (o - mean) * lax.rsqrt(var + GN_EPS)
    o = o.reshape(B, S, H * dv) * gn_gain.astype(jnp.float32)
    return o.astype(v.dtype)


def setup_inputs(seed: int = 0) -> dict:
    key = jax.random.key(seed)
    ks = jax.random.split(key, 10)
    x = jax.random.normal(ks[0], (BATCH, SEQ, D_MODEL), jnp.float32)
    norm_gain = 1.0 + 0.01 * jax.random.normal(ks[1], (DEPTH, D_MODEL), jnp.float32)
    w_in = jax.random.normal(ks[2], (DEPTH, D_MODEL, IN_WIDTH), jnp.float32) * D_MODEL ** -0.5
    rel_bias = 0.5 * jax.random.normal(ks[3], (DEPTH, A_HEADS, 2 * REL_CLIP + 1), jnp.float32)
    gn_gain = 1.0 + 0.01 * jax.random.normal(ks[4], (DEPTH, R_V_WIDTH), jnp.float32)
    w_out_attn = jax.random.normal(ks[5], (DEPTH, A_WIDTH, D_MODEL), jnp.float32) * A_WIDTH ** -0.5
    w_out_ret = jax.random.normal(ks[6], (DEPTH, R_V_WIDTH, D_MODEL), jnp.float32) * R_V_WIDTH ** -0.5
    w_out = jax.random.normal(ks[7], (DEPTH, D_MODEL, D_MODEL), jnp.float32) * D_MODEL ** -0.5
    final_gain = 1.0 + 0.01 * jax.random.normal(ks[8], (D_MODEL,), jnp.float32)
    return {"x": x, "norm_gain": norm_gain, "w_in": w_in, "rel_bias": rel_bias,
            "gn_gain": gn_gain, "w_out_attn": w_out_attn, "w_out_ret": w_out_ret,
            "w_out": w_out, "final_gain": final_gain}


def reference(x, norm_gain, w_in, rel_bias, gn_gain, w_out_attn, w_out_ret, w_out, final_gain):
    B, S, _ = x.shape
    positions = jnp.arange(S)
    for l in range(DEPTH):
        h = rms_norm(x, norm_gain[l])
        proj = jnp.einsum('bsd,df->bsf', h, w_in[l])
        q_a, k_a, v_a, g_a, q_r, k_r, v_r, g_r, m_a, m_r = split_columns(proj)

        o_a = chunk_band_attention(q_a.reshape(B, S, A_HEADS, A_HEAD_DIM),
                                   k_a.reshape(B, S, A_HEADS, A_HEAD_DIM),
                                   v_a.reshape(B, S, A_HEADS, A_HEAD_DIM),
                                   rel_bias[l]).reshape(B, S, A_WIDTH)
        branch_a = jnp.einsum('bsf,fd->bsd', o_a * jax.nn.silu(g_a), w_out_attn[l])

        qr = rotary(q_r.reshape(B, S, R_HEADS, R_KEY_DIM), positions)
        kr = rotary(k_r.reshape(B, S, R_HEADS, R_KEY_DIM), positions)
        o_r = retention_chunkwise(qr, kr, v_r.reshape(B, S, R_HEADS, R_VAL_DIM), gn_gain[l])
        branch_r = jnp.einsum('bsf,fd->bsd', o_r * jax.nn.silu(g_r), w_out_ret[l])

        mixed = jax.nn.sigmoid(m_a) * branch_a + jax.nn.sigmoid(m_r) * branch_r
        x = x + jnp.einsum('bsd,de->bse', mixed, w_out[l]).astype(x.dtype)
    return rms_norm(x, final_gain)
```

```python
import functools

import numpy as np
import jax
import jax.numpy as jnp
from jax import lax
from jax.experimental import pallas as pl
from jax.experimental.pallas import tpu as pltpu

D_MODEL = 1024
CHUNK = 64
N_LEFT_CHUNKS = 8
A_HEADS = 8
A_HEAD_DIM = 64
A_WIDTH = A_HEADS * A_HEAD_DIM
REL_CLIP = 128
R_HEADS = 8
R_KEY_DIM = 64
R_VAL_DIM = 128
R_QK_WIDTH = R_HEADS * R_KEY_DIM
R_V_WIDTH = R_HEADS * R_VAL_DIM
ROPE_BASE = 10000.0
NORM_EPS = 1e-6
GN_EPS = 1e-5
NEG_INF = -1e30
IN_WIDTH = 4 * A_WIDTH + 2 * R_QK_WIDTH + 2 * R_V_WIDTH + 2 * D_MODEL

COL_BLOCK = 512
N_COL_BLOCKS = IN_WIDTH // COL_BLOCK
COL_Q_A, COL_K_A, COL_V_A, COL_G_A, COL_Q_R, COL_K_R = 0, 1, 2, 3, 4, 5
WIDE_V_R, WIDE_G_R, WIDE_M_A, WIDE_M_R = 3, 4, 5, 6

LANES = 128
HEADS_PER_LANE_GROUP = LANES // A_HEAD_DIM

PROJ_ROWS = 512
SEQ_BLOCK = 256
CHUNKS_PER_BLOCK = SEQ_BLOCK // CHUNK
KEY_BLOCKS = N_LEFT_CHUNKS // CHUNKS_PER_BLOCK + 1
VMEM_LIMIT_BYTES = 56 * 1024 * 1024

BF16 = jnp.bfloat16
F32 = jnp.float32


def _resident(shape):
    zeros = (0,) * len(shape)
    return pl.BlockSpec(shape, lambda *_: zeros, pipeline_mode=pl.Buffered(1))


def _silu(g):
    return g * jax.nn.sigmoid(g)


def _proj_kernel(x_ref, gain_ref, w_ref, cos_ref, sin_ref, o_ref):
    x = x_ref[...]
    inv = lax.rsqrt(jnp.mean(x * x, axis=-1, keepdims=True) + NORM_EPS)
    h = (x * inv * gain_ref[...]).astype(BF16)
    lane = lax.broadcasted_iota(jnp.int32, (1, COL_BLOCK), 1)
    first_half = (lane % R_KEY_DIM) < (R_KEY_DIM // 2)
    for c in range(N_COL_BLOCKS):
        cols = slice(c * COL_BLOCK, (c + 1) * COL_BLOCK)
        acc = jnp.dot(h, w_ref[:, cols], preferred_element_type=F32)
        if c in (COL_Q_R, COL_K_R):
            partner = jnp.where(first_half,
                                pltpu.roll(acc, COL_BLOCK - R_KEY_DIM // 2, 1),
                                pltpu.roll(acc, R_KEY_DIM // 2, 1))
            acc = acc * cos_ref[...] + partner * sin_ref[...]
        if c in (COL_Q_A, COL_Q_R):
            acc = acc * (A_HEAD_DIM ** -0.5)
        o_ref[:, cols] = acc.astype(BF16)


def _input_projection(x2d, gain, w_bf16, cos_t, sin_t, seq_len):
    n_tok = x2d.shape[0]
    seq_tiles = seq_len // PROJ_ROWS
    return pl.pallas_call(
        _proj_kernel,
        grid=(n_tok // PROJ_ROWS,),
        in_specs=[
            pl.BlockSpec((PROJ_ROWS, D_MODEL), lambda i: (i, 0)),
            _resident((1, D_MODEL)),
            _resident((D_MODEL, IN_WIDTH)),
            pl.BlockSpec((PROJ_ROWS, COL_BLOCK), lambda i: (i % seq_tiles, 0)),
            pl.BlockSpec((PROJ_ROWS, COL_BLOCK), lambda i: (i % seq_tiles, 0)),
        ],
        out_specs=pl.BlockSpec((PROJ_ROWS, IN_WIDTH), lambda i: (i, 0)),
        out_shape=jax.ShapeDtypeStruct((n_tok, IN_WIDTH), BF16),
        compiler_params=pltpu.CompilerParams(
            dimension_semantics=("parallel",), vmem_limit_bytes=VMEM_LIMIT_BYTES),
        name="input_projection",
    )(x2d, gain, w_bf16, cos_t, sin_t)


def _attn_kernel(q_ref, k0_ref, k1_ref, k2_ref, v0_ref, v1_ref, v2_ref, g_ref, bias_ref, o_ref):
    j = pl.program_id(1)
    k_refs = (k0_ref, k1_ref, k2_ref)
    v_refs = (v0_ref, v1_ref, v2_ref)
    lane = lax.broadcasted_iota(jnp.int32, (1, LANES), 1)
    low_half = lane < A_HEAD_DIM
    for p in range(A_HEADS // HEADS_PER_LANE_GROUP):
        cols = slice(p * LANES, (p + 1) * LANES)
        q_pair = q_ref[:, cols]
        outs = []
        for e in range(HEADS_PER_LANE_GROUP):
            h = p * HEADS_PER_LANE_GROUP + e
            q_h = jnp.where(low_half if e == 0 else ~low_half, q_pair, jnp.zeros_like(q_pair))
            scores = []
            for b in range(KEY_BLOCKS):
                s = lax.dot_general(q_h, k_refs[b][:, cols], (((1,), (1,)), ((), ())),
                                    preferred_element_type=F32)
                s = s + bias_ref[h, :, b * SEQ_BLOCK:(b + 1) * SEQ_BLOCK]
                if b < KEY_BLOCKS - 1:
                    s = jnp.where(j + b >= KEY_BLOCKS - 1, s, NEG_INF)
                scores.append(s)
            m = functools.reduce(jnp.maximum, [jnp.max(s, axis=-1, keepdims=True) for s in scores])
            probs = [jnp.exp(s - m) for s in scores]
            denom = functools.reduce(jnp.add, [jnp.sum(pb, axis=-1, keepdims=True) for pb in probs])
            acc = functools.reduce(jnp.add, [
                jnp.dot(probs[b].astype(BF16), v_refs[b][:, cols], preferred_element_type=F32)
                for b in range(KEY_BLOCKS)])
            outs.append(acc * (1.0 / denom))
        o_pair = jnp.where(low_half, outs[0], outs[1])
        g = g_ref[:, cols].astype(F32)
        o_ref[:, cols] = (o_pair * _silu(g)).astype(BF16)


def _band_attention(proj, bias_tile, batch, seq_len):
    n_tok = proj.shape[0]
    blocks_per_seq = seq_len // SEQ_BLOCK

    def rows(b, j):
        return b * blocks_per_seq + j

    def kv_spec(col, back):
        return pl.BlockSpec((SEQ_BLOCK, COL_BLOCK),
                            lambda b, j: (rows(b, jnp.maximum(j - back, 0)), col))

    back_offsets = tuple(range(KEY_BLOCKS - 1, -1, -1))
    return pl.pallas_call(
        _attn_kernel,
        grid=(batch, blocks_per_seq),
        in_specs=[pl.BlockSpec((SEQ_BLOCK, COL_BLOCK), lambda b, j: (rows(b, j), COL_Q_A))]
        + [kv_spec(COL_K_A, back) for back in back_offsets]
        + [kv_spec(COL_V_A, back) for back in back_offsets]
        + [pl.BlockSpec((SEQ_BLOCK, COL_BLOCK), lambda b, j: (rows(b, j), COL_G_A)),
           _resident(bias_tile.shape)],
        out_specs=pl.BlockSpec((SEQ_BLOCK, A_WIDTH), lambda b, j: (rows(b, j), 0)),
        out_shape=jax.ShapeDtypeStruct((n_tok, A_WIDTH), BF16),
        compiler_params=pltpu.CompilerParams(
            dimension_semantics=("parallel", "parallel"), vmem_limit_bytes=VMEM_LIMIT_BYTES),
        name="band_attention",
    )(proj, proj, proj, proj, proj, proj, proj, proj, bias_tile)


def _retention_kernel(q_ref, k_ref, v_ref, g_ref, dmat_ref, cross_ref, sdec_ref, cdec_ref,
                      gain_ref, o_ref, state_ref):
    @pl.when(pl.program_id(1) == 0)
    def _():
        state_ref[...] = jnp.zeros_like(state_ref)

    lane = lax.broadcasted_iota(jnp.int32, (1, LANES), 1)
    low_half = lane < R_KEY_DIM
    for p in range(R_HEADS // HEADS_PER_LANE_GROUP):
        cols = slice(p * LANES, (p + 1) * LANES)
        q_pair = q_ref[:, cols]
        k_pair = k_ref[:, cols]
        state = state_ref[p]
        state_bf = state.astype(BF16)
        k_decayed = (k_pair.astype(F32) * sdec_ref[:, cols]).astype(BF16)
        q_cross = (q_pair.astype(F32) * cross_ref[:, cols]).astype(BF16)
        new_rows = []
        for e in range(HEADS_PER_LANE_GROUP):
            h = p * HEADS_PER_LANE_GROUP + e
            vcols = slice(h * R_VAL_DIM, (h + 1) * R_VAL_DIM)
            half = low_half if e == 0 else ~low_half
            q_h = jnp.where(half, q_pair, jnp.zeros_like(q_pair))
            qc_h = jnp.where(half, q_cross, jnp.zeros_like(q_cross))
            v_h = v_ref[:, vcols]
            s = lax.dot_general(q_h, k_pair, (((1,), (1,)), ((), ())),
                                preferred_element_type=F32) * dmat_ref[h]
            o = jnp.dot(s.astype(BF16), v_h, preferred_element_type=F32)
            o = o + jnp.dot(qc_h, state_bf, preferred_element_type=F32)
            kv = lax.dot_general(k_decayed, v_h, (((0,), (0,)), ((), ())),
                                 preferred_element_type=F32)
            new_rows.append(kv[e * R_KEY_DIM:(e + 1) * R_KEY_DIM])
            mean = jnp.mean(o, axis=-1, keepdims=True)
            cen = o - mean
            var = jnp.mean(cen * cen, axis=-1, keepdims=True)
            normed = cen * lax.rsqrt(var + GN_EPS) * gain_ref[:, vcols]
            g = g_ref[:, vcols].astype(F32)
            o_ref[:, vcols] = (normed * _silu(g)).astype(BF16)
        state_ref[p] = state * cdec_ref[p] + jnp.concatenate(new_rows, axis=0)


def _retention(proj, gn_gain, tables, batch, seq_len):
    n_tok = proj.shape[0]
    blocks_per_seq = seq_len // SEQ_BLOCK
    dmat, cross, sdec, cdec = tables
    n_pairs = R_HEADS // HEADS_PER_LANE_GROUP

    def rows(b, j):
        return b * blocks_per_seq + j

    return pl.pallas_call(
        _retention_kernel,
        grid=(batch, blocks_per_seq),
        in_specs=[
            pl.BlockSpec((SEQ_BLOCK, COL_BLOCK), lambda b, j: (rows(b, j), COL_Q_R)),
            pl.BlockSpec((SEQ_BLOCK, COL_BLOCK), lambda b, j: (rows(b, j), COL_K_R)),
            pl.BlockSpec((SEQ_BLOCK, R_V_WIDTH), lambda b, j: (rows(b, j), WIDE_V_R)),
            pl.BlockSpec((SEQ_BLOCK, R_V_WIDTH), lambda b, j: (rows(b, j), WIDE_G_R)),
            _resident(dmat.shape), _resident(cross.shape), _resident(sdec.shape),
            _resident(cdec.shape), _resident(gn_gain.shape),
        ],
        out_specs=pl.BlockSpec((SEQ_BLOCK, R_V_WIDTH), lambda b, j: (rows(b, j), 0)),
        out_shape=jax.ShapeDtypeStruct((n_tok, R_V_WIDTH), BF16),
        scratch_shapes=[pltpu.VMEM((n_pairs, LANES, R_VAL_DIM), F32)],
        compiler_params=pltpu.CompilerParams(
            dimension_semantics=("parallel", "arbitrary"), vmem_limit_bytes=VMEM_LIMIT_BYTES),
        name="retention",
    )(proj, proj, proj, proj, dmat, cross, sdec, cdec, gn_gain)


def _out_kernel(att_ref, ret_ref, ma_ref, mr_ref, x_ref, wa_ref, wr_ref, wo_ref, fg_ref, o_ref):
    branch_a = jnp.dot(att_ref[...], wa_ref[...], preferred_element_type=F32)
    branch_r = jnp.dot(ret_ref[...], wr_ref[...], preferred_element_type=F32)
    mixed = (jax.nn.sigmoid(ma_ref[...].astype(F32)) * branch_a
             + jax.nn.sigmoid(mr_ref[...].astype(F32)) * branch_r)
    y = x_ref[...] + jnp.dot(mixed.astype(BF16), wo_ref[...], preferred_element_type=F32)
    inv = lax.rsqrt(jnp.mean(y * y, axis=-1, keepdims=True) + NORM_EPS)
    o_ref[...] = y * inv * fg_ref[...]


def _output_block(att, ret, proj, x2d, wa, wr, wo, final_gain):
    n_tok = x2d.shape[0]
    row_spec = lambda width, col: pl.BlockSpec((PROJ_ROWS, width), lambda i: (i, col))
    return pl.pallas_call(
        _out_kernel,
        grid=(n_tok // PROJ_ROWS,),
        in_specs=[
            row_spec(A_WIDTH, 0), row_spec(R_V_WIDTH, 0),
            row_spec(D_MODEL, WIDE_M_A), row_spec(D_MODEL, WIDE_M_R),
            row_spec(D_MODEL, 0),
            _resident(wa.shape), _resident(wr.shape), _resident(wo.shape),
            _resident(final_gain.shape),
        ],
        out_specs=row_spec(D_MODEL, 0),
        out_shape=jax.ShapeDtypeStruct((n_tok, D_MODEL), F32),
        compiler_params=pltpu.CompilerParams(
            dimension_semantics=("parallel",), vmem_limit_bytes=VMEM_LIMIT_BYTES),
        name="output_block",
    )(att, ret, proj, proj, x2d, wa, wr, wo, final_gain)


def _rotary_tables(seq_len):
    half = R_KEY_DIM // 2
    inv_freq = jnp.power(ROPE_BASE, -jnp.arange(half, dtype=F32) / half)
    ang = jnp.arange(seq_len).astype(F32)[:, None] * inv_freq[None, :]
    cos, sin = jnp.cos(ang), jnp.sin(ang)
    cos_t = jnp.tile(jnp.concatenate([cos, cos], axis=-1), (1, R_HEADS))
    sin_t = jnp.tile(jnp.concatenate([-sin, sin], axis=-1), (1, R_HEADS))
    return cos_t, sin_t


def _bias_tile(rel_bias):
    qi = np.arange(SEQ_BLOCK)[:, None]
    kk = np.arange(KEY_BLOCKS * SEQ_BLOCK)[None, :]
    dist = (KEY_BLOCKS - 1) * SEQ_BLOCK + qi - kk
    idx = np.clip(dist, -REL_CLIP, REL_CLIP) + REL_CLIP
    chunk_gap = kk // CHUNK - qi // CHUNK
    in_band = (chunk_gap >= 0) & (chunk_gap <= N_LEFT_CHUNKS)
    return jnp.where(in_band[None], rel_bias[:, idx].astype(F32), NEG_INF)


def _retention_tables():
    log_gamma = jnp.log1p(-jnp.exp2(-5.0 - jnp.arange(R_HEADS, dtype=F32)))
    pos = jnp.arange(SEQ_BLOCK, dtype=F32)
    diff = pos[:, None] - pos[None, :]
    chunk_diff = (np.arange(SEQ_BLOCK)[:, None] // CHUNK) - (np.arange(SEQ_BLOCK)[None, :] // CHUNK)
    expo = jnp.where(chunk_diff == 0, jnp.abs(diff), diff)
    dmat = jnp.where((chunk_diff >= 0)[None], jnp.exp(expo[None] * log_gamma[:, None, None]), 0.0)
    per_lane = lambda tab, width: jnp.repeat(tab.T, width, axis=1)
    cross = per_lane(jnp.exp((pos + 1.0)[None, :] * log_gamma[:, None]), R_KEY_DIM)
    sdec = per_lane(jnp.exp((SEQ_BLOCK - 1.0 - pos)[None, :] * log_gamma[:, None]), R_KEY_DIM)
    block_decay = jnp.exp(SEQ_BLOCK * log_gamma)
    cdec = jnp.broadcast_to(
        jnp.repeat(block_decay, R_KEY_DIM).reshape(R_HEADS // HEADS_PER_LANE_GROUP, LANES, 1),
        (R_HEADS // HEADS_PER_LANE_GROUP, LANES, R_VAL_DIM))
    return dmat.astype(F32), cross, sdec, cdec


def kernel(x, norm_gain, w_in, rel_bias, gn_gain, w_out_attn, w_out_ret, w_out, final_gain):
    batch, seq_len, d_model = x.shape
    assert d_model == D_MODEL and norm_gain.shape[0] == 1, "single-layer block of width D_MODEL"
    assert seq_len % PROJ_ROWS == 0 and seq_len % SEQ_BLOCK == 0
    x2d = x.reshape(batch * seq_len, D_MODEL)
    cos_t, sin_t = _rotary_tables(seq_len)
    proj = _input_projection(x2d, norm_gain[0][None, :], w_in[0].astype(BF16), cos_t, sin_t, seq_len)
    att = _band_attention(proj, _bias_tile(rel_bias[0]), batch, seq_len)
    ret = _retention(proj, gn_gain[0][None, :], _retention_tables(), batch, seq_len)
    out = _output_block(att, ret, proj, x2d, w_out_attn[0].astype(BF16), w_out_ret[0].astype(BF16),
                        w_out[0].astype(BF16), final_gain[None, :])
    return out.reshape(batch, seq_len, D_MODEL)
```

```python
import functools

import numpy as np
import jax
import jax.numpy as jnp
from jax import lax
from jax.experimental import pallas as pl
from jax.experimental.pallas import tpu as pltpu

D_MODEL = 1024
CHUNK = 64
N_LEFT_CHUNKS = 8
A_HEADS = 8
A_HEAD_DIM = 64
A_WIDTH = A_HEADS * A_HEAD_DIM
REL_CLIP = 128
R_HEADS = 8
R_KEY_DIM = 64
R_VAL_DIM = 128
R_QK_WIDTH = R_HEADS * R_KEY_DIM
R_V_WIDTH = R_HEADS * R_VAL_DIM
ROPE_BASE = 10000.0
NORM_EPS = 1e-6
GN_EPS = 1e-5
NEG_INF = -1e30
LOG2_E = 1.4426950408889634
IN_WIDTH = 4 * A_WIDTH + 2 * R_QK_WIDTH + 2 * R_V_WIDTH + 2 * D_MODEL

COL_BLOCK = 512
N_COL_BLOCKS = IN_WIDTH // COL_BLOCK
COL_Q_A, COL_K_A, COL_V_A, COL_G_A, COL_Q_R, COL_K_R = 0, 1, 2, 3, 4, 5
WIDE_V_R, WIDE_G_R, WIDE_M_A, WIDE_M_R = 3, 4, 5, 6

LANES = 128
HEADS_PER_LANE_GROUP = LANES // A_HEAD_DIM

PROJ_ROWS = 512
SEQ_BLOCK = 256
CHUNKS_PER_BLOCK = SEQ_BLOCK // CHUNK
KEY_BLOCKS = N_LEFT_CHUNKS // CHUNKS_PER_BLOCK + 1
VMEM_LIMIT_BYTES = 56 * 1024 * 1024

BF16 = jnp.bfloat16
F32 = jnp.float32


def _resident(shape):
    zeros = (0,) * len(shape)
    return pl.BlockSpec(shape, lambda *_: zeros, pipeline_mode=pl.Buffered(1))


def _silu(g):
    return g * jax.nn.sigmoid(g)


def _proj_kernel(x_ref, gain_ref, w_ref, cos_ref, sin_ref, o_ref):
    x = x_ref[...]
    inv = lax.rsqrt(jnp.mean(x * x, axis=-1, keepdims=True) + NORM_EPS)
    h = (x * inv * gain_ref[...]).astype(BF16)
    lane = lax.broadcasted_iota(jnp.int32, (1, COL_BLOCK), 1)
    first_half = (lane % R_KEY_DIM) < (R_KEY_DIM // 2)
    for c in range(N_COL_BLOCKS):
        cols = slice(c * COL_BLOCK, (c + 1) * COL_BLOCK)
        acc = jnp.dot(h, w_ref[:, cols], preferred_element_type=F32)
        if c in (COL_Q_R, COL_K_R):
            partner = jnp.where(first_half,
                                pltpu.roll(acc, COL_BLOCK - R_KEY_DIM // 2, 1),
                                pltpu.roll(acc, R_KEY_DIM // 2, 1))
            acc = acc * cos_ref[...] + partner * sin_ref[...]
        if c == COL_Q_A:
            acc = acc * (A_HEAD_DIM ** -0.5 * LOG2_E)
        if c == COL_Q_R:
            acc = acc * (R_KEY_DIM ** -0.5)
        o_ref[:, cols] = acc.astype(BF16)


def _input_projection(x2d, gain, w_bf16, cos_t, sin_t, seq_len):
    n_tok = x2d.shape[0]
    seq_tiles = seq_len // PROJ_ROWS
    return pl.pallas_call(
        _proj_kernel,
        grid=(n_tok // PROJ_ROWS,),
        in_specs=[
            pl.BlockSpec((PROJ_ROWS, D_MODEL), lambda i: (i, 0)),
            _resident((1, D_MODEL)),
            _resident((D_MODEL, IN_WIDTH)),
            pl.BlockSpec((PROJ_ROWS, COL_BLOCK), lambda i: (i % seq_tiles, 0)),
            pl.BlockSpec((PROJ_ROWS, COL_BLOCK), lambda i: (i % seq_tiles, 0)),
        ],
        out_specs=pl.BlockSpec((PROJ_ROWS, IN_WIDTH), lambda i: (i, 0)),
        out_shape=jax.ShapeDtypeStruct((n_tok, IN_WIDTH), BF16),
        compiler_params=pltpu.CompilerParams(
            dimension_semantics=("parallel",), vmem_limit_bytes=VMEM_LIMIT_BYTES),
        name="input_projection",
    )(x2d, gain, w_bf16, cos_t, sin_t)


def _attn_kernel(q_ref, k0_ref, k1_ref, k2_ref, v0_ref, v1_ref, v2_ref, g_ref, bias_ref, o_ref,
                 s_ref):
    variant = jnp.minimum(pl.program_id(1), KEY_BLOCKS - 1)
    k_refs = (k0_ref, k1_ref, k2_ref)
    v_refs = (v0_ref, v1_ref, v2_ref)
    lane = lax.broadcasted_iota(jnp.int32, (1, LANES), 1)
    low_half = lane < A_HEAD_DIM
    halves = (low_half, ~low_half)
    n_pairs = A_HEADS // HEADS_PER_LANE_GROUP

    row_max = []
    for p in range(n_pairs):
        cols = slice(p * LANES, (p + 1) * LANES)
        q_pair = q_ref[:, cols]
        k_pair = jnp.concatenate([r[:, cols] for r in k_refs], axis=0)
        for e in range(HEADS_PER_LANE_GROUP):
            h = p * HEADS_PER_LANE_GROUP + e
            q_h = jnp.where(halves[e], q_pair, jnp.zeros_like(q_pair))
            s = lax.dot_general(q_h, k_pair, (((1,), (1,)), ((), ())), preferred_element_type=F32)
            s = s + bias_ref[variant, h]
            s_ref[h] = s
            row_max.append(jnp.max(s, axis=-1, keepdims=True))

    for p in range(n_pairs):
        cols = slice(p * LANES, (p + 1) * LANES)
        v_pair = jnp.concatenate([r[:, cols] for r in v_refs], axis=0)
        outs = []
        for e in range(HEADS_PER_LANE_GROUP):
            h = p * HEADS_PER_LANE_GROUP + e
            probs = jnp.exp2(s_ref[h] - row_max[h]).astype(BF16)
            v_h = jnp.where(halves[e], v_pair, jnp.ones_like(v_pair))
            acc = jnp.dot(probs, jnp.concatenate([v_h, v_h], axis=1),
                          preferred_element_type=F32)[:, :LANES]
            outs.append(acc / pltpu.roll(acc, A_HEAD_DIM, 1))
        o_pair = jnp.where(low_half, outs[0], outs[1])
        g = g_ref[:, cols].astype(F32)
        o_ref[:, cols] = (o_pair * _silu(g)).astype(BF16)


def _band_attention(proj, bias_tile, batch, seq_len):
    n_tok = proj.shape[0]
    blocks_per_seq = seq_len // SEQ_BLOCK

    def rows(b, j):
        return b * blocks_per_seq + j

    def kv_spec(col, back):
        return pl.BlockSpec((SEQ_BLOCK, COL_BLOCK),
                            lambda b, j: (rows(b, jnp.maximum(j - back, 0)), col))

    back_offsets = tuple(range(KEY_BLOCKS - 1, -1, -1))
    return pl.pallas_call(
        _attn_kernel,
        grid=(batch, blocks_per_seq),
        in_specs=[pl.BlockSpec((SEQ_BLOCK, COL_BLOCK), lambda b, j: (rows(b, j), COL_Q_A))]
        + [kv_spec(COL_K_A, back) for back in back_offsets]
        + [kv_spec(COL_V_A, back) for back in back_offsets]
        + [pl.BlockSpec((SEQ_BLOCK, COL_BLOCK), lambda b, j: (rows(b, j), COL_G_A)),
           _resident(bias_tile.shape)],
        out_specs=pl.BlockSpec((SEQ_BLOCK, A_WIDTH), lambda b, j: (rows(b, j), 0)),
        out_shape=jax.ShapeDtypeStruct((n_tok, A_WIDTH), BF16),
        scratch_shapes=[pltpu.VMEM((A_HEADS, SEQ_BLOCK, KEY_BLOCKS * SEQ_BLOCK), F32)],
        compiler_params=pltpu.CompilerParams(
            dimension_semantics=("parallel", "parallel"), vmem_limit_bytes=VMEM_LIMIT_BYTES),
        name="band_attention",
    )(proj, proj, proj, proj, proj, proj, proj, proj, bias_tile)


def _retention_kernel(q_ref, k_ref, v_ref, g_ref, dmat_ref, cross_ref, sdec_ref, cdec_ref,
                      gain_ref, o_ref, state_ref):
    @pl.when(pl.program_id(1) == 0)
    def _():
        state_ref[...] = jnp.zeros_like(state_ref)

    lane = lax.broadcasted_iota(jnp.int32, (1, LANES), 1)
    low_half = lane < R_KEY_DIM
    for p in range(R_HEADS // HEADS_PER_LANE_GROUP):
        cols = slice(p * LANES, (p + 1) * LANES)
        q_pair = q_ref[:, cols]
        k_pair = k_ref[:, cols]
        state = state_ref[p]
        state_bf = state.astype(BF16)
        k_decayed = (k_pair.astype(F32) * sdec_ref[:, cols]).astype(BF16)
        q_cross = (q_pair.astype(F32) * cross_ref[:, cols]).astype(BF16)
        new_rows = []
        for e in range(HEADS_PER_LANE_GROUP):
            h = p * HEADS_PER_LANE_GROUP + e
            vcols = slice(h * R_VAL_DIM, (h + 1) * R_VAL_DIM)
            half = low_half if e == 0 else ~low_half
            q_h = jnp.where(half, q_pair, jnp.zeros_like(q_pair))
            qc_h = jnp.where(half, q_cross, jnp.zeros_like(q_cross))
            v_h = v_ref[:, vcols]
            s = lax.dot_general(q_h, k_pair, (((1,), (1,)), ((), ())),
                                preferred_element_type=F32) * dmat_ref[h]
            o = jnp.dot(s.astype(BF16), v_h, preferred_element_type=F32)
            o = o + jnp.dot(qc_h, state_bf, preferred_element_type=F32)
            kv = lax.dot_general(k_decayed, v_h, (((0,), (0,)), ((), ())),
                                 preferred_element_type=F32)
            new_rows.append(kv[e * R_KEY_DIM:(e + 1) * R_KEY_DIM])
            mean = jnp.mean(o, axis=-1, keepdims=True)
            cen = o - mean
            var = jnp.mean(cen * cen, axis=-1, keepdims=True)
            normed = cen * lax.rsqrt(var + GN_EPS) * gain_ref[:, vcols]
            g = g_ref[:, vcols].astype(F32)
            o_ref[:, vcols] = (normed * _silu(g)).astype(BF16)
        state_ref[p] = state * cdec_ref[p] + jnp.concatenate(new_rows, axis=0)


def _retention(proj, gn_gain, tables, batch, seq_len):
    n_tok = proj.shape[0]
    blocks_per_seq = seq_len // SEQ_BLOCK
    dmat, cross, sdec, cdec = tables
    n_pairs = R_HEADS // HEADS_PER_LANE_GROUP

    def rows(b, j):
        return b * blocks_per_seq + j

    return pl.pallas_call(
        _retention_kernel,
        grid=(batch, blocks_per_seq),
        in_specs=[
            pl.BlockSpec((SEQ_BLOCK, COL_BLOCK), lambda b, j: (rows(b, j), COL_Q_R)),
            pl.BlockSpec((SEQ_BLOCK, COL_BLOCK), lambda b, j: (rows(b, j), COL_K_R)),
            pl.BlockSpec((SEQ_BLOCK, R_V_WIDTH), lambda b, j: (rows(b, j), WIDE_V_R)),
            pl.BlockSpec((SEQ_BLOCK, R_V_WIDTH), lambda b, j: (rows(b, j), WIDE_G_R)),
            _resident(dmat.shape), _resident(cross.shape), _resident(sdec.shape),
            _resident(cdec.shape), _resident(gn_gain.shape),
        ],
        out_specs=pl.BlockSpec((SEQ_BLOCK, R_V_WIDTH), lambda b, j: (rows(b, j), 0)),
        out_shape=jax.ShapeDtypeStruct((n_tok, R_V_WIDTH), BF16),
        scratch_shapes=[pltpu.VMEM((n_pairs, LANES, R_VAL_DIM), F32)],
        compiler_params=pltpu.CompilerParams(
            dimension_semantics=("parallel", "arbitrary"), vmem_limit_bytes=VMEM_LIMIT_BYTES),
        name="retention",
    )(proj, proj, proj, proj, dmat, cross, sdec, cdec, gn_gain)


def _out_kernel(att_ref, ret_ref, ma_ref, mr_ref, x_ref, wa_ref, wr_ref, wo_ref, fg_ref, o_ref):
    branch_a = jnp.dot(att_ref[...], wa_ref[...], preferred_element_type=F32)
    branch_r = jnp.dot(ret_ref[...], wr_ref[...], preferred_element_type=F32)
    mixed = (jax.nn.sigmoid(ma_ref[...].astype(F32)) * branch_a
             + jax.nn.sigmoid(mr_ref[...].astype(F32)) * branch_r)
    y = x_ref[...] + jnp.dot(mixed.astype(BF16), wo_ref[...], preferred_element_type=F32)
    inv = lax.rsqrt(jnp.mean(y * y, axis=-1, keepdims=True) + NORM_EPS)
    o_ref[...] = y * inv * fg_ref[...]


def _output_block(att, ret, proj, x2d, wa, wr, wo, final_gain):
    n_tok = x2d.shape[0]
    row_spec = lambda width, col: pl.BlockSpec((PROJ_ROWS, width), lambda i: (i, col))
    return pl.pallas_call(
        _out_kernel,
        grid=(n_tok // PROJ_ROWS,),
        in_specs=[
            row_spec(A_WIDTH, 0), row_spec(R_V_WIDTH, 0),
            row_spec(D_MODEL, WIDE_M_A), row_spec(D_MODEL, WIDE_M_R),
            row_spec(D_MODEL, 0),
            _resident(wa.shape), _resident(wr.shape), _resident(wo.shape),
            _resident(final_gain.shape),
        ],
        out_specs=row_spec(D_MODEL, 0),
        out_shape=jax.ShapeDtypeStruct((n_tok, D_MODEL), F32),
        compiler_params=pltpu.CompilerParams(
            dimension_semantics=("parallel",), vmem_limit_bytes=VMEM_LIMIT_BYTES),
        name="output_block",
    )(att, ret, proj, proj, x2d, wa, wr, wo, final_gain)


def _rotary_tables(seq_len):
    half = R_KEY_DIM // 2
    inv_freq = jnp.power(ROPE_BASE, -jnp.arange(half, dtype=F32) / half)
    ang = jnp.arange(seq_len).astype(F32)[:, None] * inv_freq[None, :]
    cos, sin = jnp.cos(ang), jnp.sin(ang)
    cos_t = jnp.tile(jnp.concatenate([cos, cos], axis=-1), (1, R_HEADS))
    sin_t = jnp.tile(jnp.concatenate([-sin, sin], axis=-1), (1, R_HEADS))
    return cos_t, sin_t


def _bias_tile(rel_bias):
    n_keys = KEY_BLOCKS * SEQ_BLOCK
    length = n_keys + SEQ_BLOCK - 1
    below = SEQ_BLOCK - 1 - REL_CLIP
    above = length - below - (2 * REL_CLIP + 1)
    heads = rel_bias.shape[0]
    by_dist = jnp.concatenate([jnp.broadcast_to(rel_bias[:, :1], (heads, below)), rel_bias,
                               jnp.broadcast_to(rel_bias[:, -1:], (heads, above))], axis=1)
    row = jnp.roll(by_dist[:, ::-1], -(SEQ_BLOCK - 1), axis=1)
    tile = jnp.tile(row, (1, SEQ_BLOCK))[:, :SEQ_BLOCK * (length - 1)]
    tile = tile.reshape(heads, SEQ_BLOCK, length - 1)[:, :, :n_keys].astype(F32) * LOG2_E
    qi = np.arange(SEQ_BLOCK)[:, None]
    kk = np.arange(n_keys)[None, :]
    chunk_gap = kk // CHUNK - qi // CHUNK
    in_band = (chunk_gap >= 0) & (chunk_gap <= N_LEFT_CHUNKS)
    variants = [in_band & (kk >= (KEY_BLOCKS - 1 - v) * SEQ_BLOCK) for v in range(KEY_BLOCKS)]
    return jnp.where(np.stack(variants)[:, None], tile[None], NEG_INF)


def _retention_tables():
    log_gamma = jnp.log1p(-jnp.exp2(-5.0 - jnp.arange(R_HEADS, dtype=F32)))
    pos = jnp.arange(SEQ_BLOCK, dtype=F32)
    diff = pos[:, None] - pos[None, :]
    chunk_diff = (np.arange(SEQ_BLOCK)[:, None] // CHUNK) - (np.arange(SEQ_BLOCK)[None, :] // CHUNK)
    expo = jnp.where(chunk_diff == 0, jnp.abs(diff), diff)
    dmat = jnp.where((chunk_diff >= 0)[None], jnp.exp(expo[None] * log_gamma[:, None, None]), 0.0)
    per_lane = lambda tab, width: jnp.repeat(tab.T, width, axis=1)
    cross = per_lane(jnp.exp((pos + 1.0)[None, :] * log_gamma[:, None]), R_KEY_DIM)
    sdec = per_lane(jnp.exp((SEQ_BLOCK - 1.0 - pos)[None, :] * log_gamma[:, None]), R_KEY_DIM)
    block_decay = jnp.exp(SEQ_BLOCK * log_gamma)
    cdec = jnp.broadcast_to(
        jnp.repeat(block_decay, R_KEY_DIM).reshape(R_HEADS // HEADS_PER_LANE_GROUP, LANES, 1),
        (R_HEADS // HEADS_PER_LANE_GROUP, LANES, R_VAL_DIM))
    return dmat.astype(F32), cross, sdec, cdec


def kernel(x, norm_gain, w_in, rel_bias, gn_gain, w_out_attn, w_out_ret, w_out, final_gain):
    batch, seq_len, d_model = x.shape
    assert d_model == D_MODEL and norm_gain.shape[0] == 1, "single-layer block of width D_MODEL"
    assert seq_len % PROJ_ROWS == 0 and seq_len % SEQ_BLOCK == 0
    x2d = x.reshape(batch * seq_len, D_MODEL)
    cos_t, sin_t = _rotary_tables(seq_len)
    proj = _input_projection(x2d, norm_gain[0][None, :], w_in[0].astype(BF16), cos_t, sin_t, seq_len)
    att = _band_attention(proj, _bias_tile(rel_bias[0]), batch, seq_len)
    ret = _retention(proj, gn_gain[0][None, :], _retention_tables(), batch, seq_len)
    out = _output_block(att, ret, proj, x2d, w_out_attn[0].astype(BF16), w_out_ret[0].astype(BF16),
                        w_out[0].astype(BF16), final_gain[None, :])
    return out.reshape(batch, seq_len, D_MODEL)
```

```python
import functools

import numpy as np
import jax
import jax.numpy as jnp
from jax import lax
from jax.experimental import pallas as pl
from jax.experimental.pallas import tpu as pltpu

D_MODEL = 1024
CHUNK = 64
N_LEFT_CHUNKS = 8
A_HEADS = 8
A_HEAD_DIM = 64
A_WIDTH = A_HEADS * A_HEAD_DIM
REL_CLIP = 128
R_HEADS = 8
R_KEY_DIM = 64
R_VAL_DIM = 128
R_QK_WIDTH = R_HEADS * R_KEY_DIM
R_V_WIDTH = R_HEADS * R_VAL_DIM
ROPE_BASE = 10000.0
NORM_EPS = 1e-6
GN_EPS = 1e-5
NEG_INF = -1e30
LOG2_E = 1.4426950408889634
IN_WIDTH = 4 * A_WIDTH + 2 * R_QK_WIDTH + 2 * R_V_WIDTH + 2 * D_MODEL

COL_BLOCK = 512
N_COL_BLOCKS = IN_WIDTH // COL_BLOCK
COL_Q_A, COL_K_A, COL_V_A, COL_G_A, COL_Q_R, COL_K_R = 0, 1, 2, 3, 4, 5
COL_Q_CROSS, COL_K_DECAYED = N_COL_BLOCKS, N_COL_BLOCKS + 1
PROJ_OUT_WIDTH = IN_WIDTH + 2 * COL_BLOCK
WIDE_V_R, WIDE_G_R, WIDE_M_A, WIDE_M_R = 3, 4, 5, 6

LANES = 128
HEADS_PER_LANE_GROUP = LANES // A_HEAD_DIM

PROJ_ROWS = 512
ROW_SPLIT = 2
SEQ_BLOCK = 256
CHUNKS_PER_BLOCK = SEQ_BLOCK // CHUNK
RET_SUB = 4
KEY_BLOCKS = N_LEFT_CHUNKS // CHUNKS_PER_BLOCK + 1
VMEM_LIMIT_BYTES = 56 * 1024 * 1024

BF16 = jnp.bfloat16
F32 = jnp.float32


def _resident(shape):
    zeros = (0,) * len(shape)
    return pl.BlockSpec(shape, lambda *_: zeros, pipeline_mode=pl.Buffered(1))


def _sigmoid(x):
    return 0.5 * jnp.tanh(0.5 * x) + 0.5


def _silu(g):
    half = 0.5 * g
    return half * jnp.tanh(half) + half


def _proj_kernel(x_ref, gain_ref, w_ref, cos_ref, sin_ref, cross_ref, sdec_ref, o_ref):
    lane = lax.broadcasted_iota(jnp.int32, (1, COL_BLOCK), 1)
    first_half = (lane % R_KEY_DIM) < (R_KEY_DIM // 2)
    for r in range(ROW_SPLIT):
        rows = slice(r * PROJ_ROWS // ROW_SPLIT, (r + 1) * PROJ_ROWS // ROW_SPLIT)
        x = x_ref[rows, :]
        inv = lax.rsqrt(jnp.mean(x * x, axis=-1, keepdims=True) + NORM_EPS)
        h = (x * inv * gain_ref[...]).astype(BF16)
        for c in range(N_COL_BLOCKS):
            cols = slice(c * COL_BLOCK, (c + 1) * COL_BLOCK)
            acc = jnp.dot(h, w_ref[:, cols], preferred_element_type=F32)
            if c in (COL_Q_R, COL_K_R):
                partner = jnp.where(first_half,
                                    pltpu.roll(acc, COL_BLOCK - R_KEY_DIM // 2, 1),
                                    pltpu.roll(acc, R_KEY_DIM // 2, 1))
                groups = COL_BLOCK // LANES
                cos = jnp.concatenate([cos_ref[rows, :]] * groups, axis=1)
                sin = jnp.concatenate([sin_ref[rows, :]] * groups, axis=1)
                acc = acc * cos + partner * sin
            if c == COL_Q_A:
                acc = acc * (A_HEAD_DIM ** -0.5 * LOG2_E)
            if c == COL_Q_R:
                acc = acc * (R_KEY_DIM ** -0.5)
                o_ref[rows, _block_cols(COL_Q_CROSS)] = (acc * cross_ref[...]).astype(BF16)
            if c == COL_K_R:
                o_ref[rows, _block_cols(COL_K_DECAYED)] = (acc * sdec_ref[...]).astype(BF16)
            o_ref[rows, cols] = acc.astype(BF16)


def _block_cols(c):
    return slice(c * COL_BLOCK, (c + 1) * COL_BLOCK)


def _input_projection(x2d, gain, w_bf16, cos_t, sin_t, cross, sdec, seq_len):
    assert PROJ_ROWS // ROW_SPLIT == SEQ_BLOCK, "a row group is one retention block"
    n_tok = x2d.shape[0]
    seq_tiles = seq_len // PROJ_ROWS
    return pl.pallas_call(
        _proj_kernel,
        grid=(n_tok // PROJ_ROWS,),
        in_specs=[
            pl.BlockSpec((PROJ_ROWS, D_MODEL), lambda i: (i, 0)),
            _resident((1, D_MODEL)),
            _resident((D_MODEL, IN_WIDTH)),
            pl.BlockSpec((PROJ_ROWS, LANES), lambda i: (i % seq_tiles, 0)),
            pl.BlockSpec((PROJ_ROWS, LANES), lambda i: (i % seq_tiles, 0)),
            _resident(cross.shape), _resident(sdec.shape),
        ],
        out_specs=pl.BlockSpec((PROJ_ROWS, PROJ_OUT_WIDTH), lambda i: (i, 0)),
        out_shape=jax.ShapeDtypeStruct((n_tok, PROJ_OUT_WIDTH), BF16),
        compiler_params=pltpu.CompilerParams(
            dimension_semantics=("parallel",), vmem_limit_bytes=VMEM_LIMIT_BYTES),
        name="input_projection",
    )(x2d, gain, w_bf16, cos_t, sin_t, cross, sdec)


def _attn_kernel(q_ref, k0_ref, k1_ref, k2_ref, v0_ref, v1_ref, v2_ref, g_ref, bias_ref, o_ref,
                 *s_refs):
    variant = jnp.minimum(pl.program_id(1), KEY_BLOCKS - 1)
    k_refs = (k0_ref, k1_ref, k2_ref)
    v_refs = (v0_ref, v1_ref, v2_ref)
    lane = lax.broadcasted_iota(jnp.int32, (1, LANES), 1)
    low_half = lane < A_HEAD_DIM
    halves = (low_half, ~low_half)
    n_pairs = A_HEADS // HEADS_PER_LANE_GROUP

    row_max = []
    for p in range(n_pairs):
        cols = slice(p * LANES, (p + 1) * LANES)
        q_pair = q_ref[:, cols]
        k_pair = jnp.concatenate([r[:, cols] for r in k_refs], axis=0)
        for e in range(HEADS_PER_LANE_GROUP):
            h = p * HEADS_PER_LANE_GROUP + e
            q_h = jnp.where(halves[e], q_pair, jnp.zeros_like(q_pair))
            s = lax.dot_general(q_h, k_pair, (((1,), (1,)), ((), ())), preferred_element_type=F32)
            s = s + bias_ref[variant, h]
            s_refs[h][0] = s
            row_max.append(jnp.max(s, axis=-1, keepdims=True))

    runtime_zero = jnp.minimum(pl.program_id(1), 0)
    for p in range(n_pairs):
        cols = slice(p * LANES, (p + 1) * LANES)
        v_pair = jnp.concatenate([r[:, cols] for r in v_refs], axis=0)
        outs = []
        for e in range(HEADS_PER_LANE_GROUP):
            h = p * HEADS_PER_LANE_GROUP + e
            probs = jnp.exp2(s_refs[h][runtime_zero] - row_max[h]).astype(BF16)
            v_h = jnp.where(halves[e], v_pair, jnp.ones_like(v_pair))
            acc = jnp.dot(probs, jnp.concatenate([v_h, v_h], axis=1),
                          preferred_element_type=F32)[:, :LANES]
            outs.append(acc / pltpu.roll(acc, A_HEAD_DIM, 1))
        o_pair = jnp.where(low_half, outs[0], outs[1])
        g = g_ref[:, cols].astype(F32)
        o_ref[:, cols] = (o_pair * _silu(g)).astype(BF16)


def _band_attention(proj, bias_tile, batch, seq_len):
    n_tok = proj.shape[0]
    blocks_per_seq = seq_len // SEQ_BLOCK

    def rows(b, j):
        return b * blocks_per_seq + j

    def kv_spec(col, back):
        return pl.BlockSpec((SEQ_BLOCK, COL_BLOCK),
                            lambda b, j: (rows(b, jnp.maximum(j - back, 0)), col))

    back_offsets = tuple(range(KEY_BLOCKS - 1, -1, -1))
    return pl.pallas_call(
        _attn_kernel,
        grid=(batch, blocks_per_seq),
        in_specs=[pl.BlockSpec((SEQ_BLOCK, COL_BLOCK), lambda b, j: (rows(b, j), COL_Q_A))]
        + [kv_spec(COL_K_A, back) for back in back_offsets]
        + [kv_spec(COL_V_A, back) for back in back_offsets]
        + [pl.BlockSpec((SEQ_BLOCK, COL_BLOCK), lambda b, j: (rows(b, j), COL_G_A)),
           _resident(bias_tile.shape)],
        out_specs=pl.BlockSpec((SEQ_BLOCK, A_WIDTH), lambda b, j: (rows(b, j), 0)),
        out_shape=jax.ShapeDtypeStruct((n_tok, A_WIDTH), BF16),
        scratch_shapes=[pltpu.VMEM((1, SEQ_BLOCK, KEY_BLOCKS * SEQ_BLOCK), F32)] * A_HEADS,
        compiler_params=pltpu.CompilerParams(
            dimension_semantics=("parallel", "parallel"), vmem_limit_bytes=VMEM_LIMIT_BYTES),
        name="band_attention",
    )(proj, proj, proj, proj, proj, proj, proj, proj, bias_tile)


def _retention_kernel(q_ref, k_ref, qc_ref, kd_ref, v_ref, g_ref, dmat_ref, cdec_ref,
                      gain_ref, o_ref, state_ref):
    @pl.when(pl.program_id(1) == 0)
    def _():
        state_ref[...] = jnp.zeros_like(state_ref)

    n_pairs = R_HEADS // HEADS_PER_LANE_GROUP
    pair_lanes = HEADS_PER_LANE_GROUP * R_VAL_DIM
    low_half = lax.broadcasted_iota(jnp.int32, (1, LANES), 1) < R_KEY_DIM
    even_lanes = lax.broadcasted_iota(jnp.int32, (1, pair_lanes), 1) < R_VAL_DIM
    own_block = ((lax.broadcasted_iota(jnp.int32, (LANES, pair_lanes), 0) < R_KEY_DIM)
                 == (lax.broadcasted_iota(jnp.int32, (LANES, pair_lanes), 1) < R_VAL_DIM))
    states = [state_ref[p] for p in range(n_pairs)]
    for sb in range(RET_SUB):
        rows = slice(sb * SEQ_BLOCK, (sb + 1) * SEQ_BLOCK)
        for p in range(n_pairs):
            cols = slice(p * LANES, (p + 1) * LANES)
            vcols = slice(p * pair_lanes, (p + 1) * pair_lanes)
            q_pair = q_ref[rows, cols]
            k_pair = k_ref[rows, cols]
            v_pair = v_ref[rows, vcols]
            q_cross = qc_ref[rows, cols]
            k_decayed = kd_ref[rows, cols]
            k_zero = jnp.zeros_like(k_pair)
            k_split = jnp.concatenate([jnp.where(low_half, k_pair, k_zero),
                                       jnp.where(low_half, k_zero, k_pair)], axis=0)
            s = lax.dot_general(q_pair, k_split, (((1,), (1,)), ((), ())),
                                preferred_element_type=F32) * dmat_ref[p]
            v_zero = jnp.zeros_like(v_pair)
            v_diag = jnp.concatenate([jnp.where(even_lanes, v_pair, v_zero),
                                      jnp.where(even_lanes, v_zero, v_pair)], axis=0)
            lhs = jnp.concatenate([s.astype(BF16), q_cross], axis=1)
            rhs = jnp.concatenate([v_diag, states[p].astype(BF16)], axis=0)
            o_pair = jnp.dot(lhs, rhs, preferred_element_type=F32)
            kv = lax.dot_general(k_decayed, v_pair, (((0,), (0,)), ((), ())),
                                 preferred_element_type=F32)
            states[p] = states[p] * cdec_ref[p] + jnp.where(own_block, kv, 0.0)
            for e in range(HEADS_PER_LANE_GROUP):
                h = p * HEADS_PER_LANE_GROUP + e
                hcols = slice(h * R_VAL_DIM, (h + 1) * R_VAL_DIM)
                o = o_pair[:, e * R_VAL_DIM:(e + 1) * R_VAL_DIM]
                mean = jnp.mean(o, axis=-1, keepdims=True)
                cen = o - mean
                var = jnp.mean(cen * cen, axis=-1, keepdims=True)
                normed = cen * lax.rsqrt(var + GN_EPS) * gain_ref[:, hcols]
                g = g_ref[rows, hcols].astype(F32)
                o_ref[rows, hcols] = (normed * _silu(g)).astype(BF16)
    for p in range(n_pairs):
        state_ref[p] = states[p]


def _retention(proj, gn_gain, dmat, cdec, batch, seq_len):
    n_tok = proj.shape[0]
    step_rows = RET_SUB * SEQ_BLOCK
    steps_per_seq = seq_len // step_rows
    n_pairs = R_HEADS // HEADS_PER_LANE_GROUP

    def rows(b, j):
        return b * steps_per_seq + j

    return pl.pallas_call(
        _retention_kernel,
        grid=(batch, steps_per_seq),
        in_specs=[
            pl.BlockSpec((step_rows, COL_BLOCK), lambda b, j: (rows(b, j), COL_Q_R)),
            pl.BlockSpec((step_rows, COL_BLOCK), lambda b, j: (rows(b, j), COL_K_R)),
            pl.BlockSpec((step_rows, COL_BLOCK), lambda b, j: (rows(b, j), COL_Q_CROSS)),
            pl.BlockSpec((step_rows, COL_BLOCK), lambda b, j: (rows(b, j), COL_K_DECAYED)),
            pl.BlockSpec((step_rows, R_V_WIDTH), lambda b, j: (rows(b, j), WIDE_V_R)),
            pl.BlockSpec((step_rows, R_V_WIDTH), lambda b, j: (rows(b, j), WIDE_G_R)),
            _resident(dmat.shape), _resident(cdec.shape), _resident(gn_gain.shape),
        ],
        out_specs=pl.BlockSpec((step_rows, R_V_WIDTH), lambda b, j: (rows(b, j), 0)),
        out_shape=jax.ShapeDtypeStruct((n_tok, R_V_WIDTH), BF16),
        scratch_shapes=[pltpu.VMEM((n_pairs, LANES, HEADS_PER_LANE_GROUP * R_VAL_DIM), F32)],
        compiler_params=pltpu.CompilerParams(
            dimension_semantics=("parallel", "arbitrary"), vmem_limit_bytes=VMEM_LIMIT_BYTES),
        name="retention",
    )(proj, proj, proj, proj, proj, proj, dmat, cdec, gn_gain)


def _out_kernel(att_ref, ret_ref, ma_ref, mr_ref, x_ref, wa_ref, wr_ref, wo_ref, fg_ref, o_ref):
    for r in range(ROW_SPLIT):
        rows = slice(r * PROJ_ROWS // ROW_SPLIT, (r + 1) * PROJ_ROWS // ROW_SPLIT)
        branch_a = jnp.dot(att_ref[rows, :], wa_ref[...], preferred_element_type=F32)
        branch_r = jnp.dot(ret_ref[rows, :], wr_ref[...], preferred_element_type=F32)
        mixed = (_sigmoid(ma_ref[rows, :].astype(F32)) * branch_a
                 + _sigmoid(mr_ref[rows, :].astype(F32)) * branch_r)
        y = x_ref[rows, :] + jnp.dot(mixed.astype(BF16), wo_ref[...], preferred_element_type=F32)
        inv = lax.rsqrt(jnp.mean(y * y, axis=-1, keepdims=True) + NORM_EPS)
        o_ref[rows, :] = y * inv * fg_ref[...]


def _output_block(att, ret, proj, x2d, wa, wr, wo, final_gain):
    n_tok = x2d.shape[0]
    row_spec = lambda width, col: pl.BlockSpec((PROJ_ROWS, width), lambda i: (i, col))
    return pl.pallas_call(
        _out_kernel,
        grid=(n_tok // PROJ_ROWS,),
        in_specs=[
            row_spec(A_WIDTH, 0), row_spec(R_V_WIDTH, 0),
            row_spec(D_MODEL, WIDE_M_A), row_spec(D_MODEL, WIDE_M_R),
            row_spec(D_MODEL, 0),
            _resident(wa.shape), _resident(wr.shape), _resident(wo.shape),
            _resident(final_gain.shape),
        ],
        out_specs=row_spec(D_MODEL, 0),
        out_shape=jax.ShapeDtypeStruct((n_tok, D_MODEL), F32),
        compiler_params=pltpu.CompilerParams(
            dimension_semantics=("parallel",), vmem_limit_bytes=VMEM_LIMIT_BYTES),
        name="output_block",
    )(att, ret, proj, proj, x2d, wa, wr, wo, final_gain)


def _rotary_tables(seq_len):
    half = R_KEY_DIM // 2
    inv_freq = jnp.power(ROPE_BASE, -jnp.arange(half, dtype=F32) / half)
    ang = jnp.arange(seq_len).astype(F32)[:, None] * inv_freq[None, :]
    cos, sin = jnp.cos(ang), jnp.sin(ang)
    cos_t = jnp.tile(jnp.concatenate([cos, cos], axis=-1), (1, HEADS_PER_LANE_GROUP))
    sin_t = jnp.tile(jnp.concatenate([-sin, sin], axis=-1), (1, HEADS_PER_LANE_GROUP))
    return cos_t, sin_t


def _bias_tile(rel_bias):
    n_keys = KEY_BLOCKS * SEQ_BLOCK
    length = n_keys + SEQ_BLOCK - 1
    below = SEQ_BLOCK - 1 - REL_CLIP
    above = length - below - (2 * REL_CLIP + 1)
    heads = rel_bias.shape[0]
    by_dist = jnp.concatenate([jnp.broadcast_to(rel_bias[:, :1], (heads, below)), rel_bias,
                               jnp.broadcast_to(rel_bias[:, -1:], (heads, above))], axis=1)
    row = jnp.roll(by_dist[:, ::-1], -(SEQ_BLOCK - 1), axis=1)
    tile = jnp.tile(row, (1, SEQ_BLOCK))[:, :SEQ_BLOCK * (length - 1)]
    tile = tile.reshape(heads, SEQ_BLOCK, length - 1)[:, :, :n_keys].astype(F32) * LOG2_E
    qi = np.arange(SEQ_BLOCK)[:, None]
    kk = np.arange(n_keys)[None, :]
    chunk_gap = kk // CHUNK - qi // CHUNK
    in_band = (chunk_gap >= 0) & (chunk_gap <= N_LEFT_CHUNKS)
    variants = [in_band & (kk >= (KEY_BLOCKS - 1 - v) * SEQ_BLOCK) for v in range(KEY_BLOCKS)]
    return jnp.where(np.stack(variants)[:, None], tile[None], NEG_INF)


def _retention_tables():
    n_pairs = R_HEADS // HEADS_PER_LANE_GROUP
    log_gamma = jnp.log1p(-jnp.exp2(-5.0 - jnp.arange(R_HEADS, dtype=F32)))
    pos = jnp.arange(SEQ_BLOCK, dtype=F32)
    diff = pos[:, None] - pos[None, :]
    chunk_diff = (np.arange(SEQ_BLOCK)[:, None] // CHUNK) - (np.arange(SEQ_BLOCK)[None, :] // CHUNK)
    expo = jnp.where(chunk_diff == 0, jnp.abs(diff), diff)
    dmat = jnp.where((chunk_diff >= 0)[None], jnp.exp(expo[None] * log_gamma[:, None, None]), 0.0)
    dmat = dmat.reshape(n_pairs, HEADS_PER_LANE_GROUP, SEQ_BLOCK, SEQ_BLOCK)
    dmat = jnp.concatenate([dmat[:, e] for e in range(HEADS_PER_LANE_GROUP)], axis=-1)
    per_lane = lambda tab, width: jnp.repeat(tab.T, width, axis=1)
    cross = per_lane(jnp.exp((pos + 1.0)[None, :] * log_gamma[:, None]), R_KEY_DIM)
    sdec = per_lane(jnp.exp((SEQ_BLOCK - 1.0 - pos)[None, :] * log_gamma[:, None]), R_KEY_DIM)
    block_decay = jnp.exp(SEQ_BLOCK * log_gamma)
    cdec = jnp.broadcast_to(jnp.repeat(block_decay, R_KEY_DIM).reshape(n_pairs, LANES, 1),
                            (n_pairs, LANES, HEADS_PER_LANE_GROUP * R_VAL_DIM))
    return dmat.astype(F32), cross, sdec, cdec


def kernel(x, norm_gain, w_in, rel_bias, gn_gain, w_out_attn, w_out_ret, w_out, final_gain):
    batch, seq_len, d_model = x.shape
    assert d_model == D_MODEL and norm_gain.shape[0] == 1, "single-layer block of width D_MODEL"
    assert seq_len % PROJ_ROWS == 0 and seq_len % (RET_SUB * SEQ_BLOCK) == 0
    x2d = x.reshape(batch * seq_len, D_MODEL)
    cos_t, sin_t = _rotary_tables(seq_len)
    dmat, cross, sdec, cdec = _retention_tables()
    proj = _input_projection(x2d, norm_gain[0][None, :], w_in[0].astype(BF16), cos_t, sin_t,
                             cross, sdec, seq_len)
    att = _band_attention(proj, _bias_tile(rel_bias[0]), batch, seq_len)
    ret = _retention(proj, gn_gain[0][None, :], dmat, cdec, batch, seq_len)
    out = _output_block(att, ret, proj, x2d, w_out_attn[0].astype(BF16), w_out_ret[0].astype(BF16),
                        w_out[0].astype(BF16), final_gain[None, :])
    return out.reshape(batch, seq_len, D_MODEL)
```

```python
import numpy as np
import jax
import jax.numpy as jnp
from jax import lax
from jax.experimental import pallas as pl
from jax.experimental.pallas import tpu as pltpu

D_MODEL = 1024
CHUNK = 64
N_LEFT_CHUNKS = 8
A_HEADS = 8
A_HEAD_DIM = 64
A_WIDTH = A_HEADS * A_HEAD_DIM
REL_CLIP = 128
R_HEADS = 8
R_KEY_DIM = 64
R_VAL_DIM = 128
R_QK_WIDTH = R_HEADS * R_KEY_DIM
R_V_WIDTH = R_HEADS * R_VAL_DIM
ROPE_BASE = 10000.0
NORM_EPS = 1e-6
GN_EPS = 1e-5
NEG_INF = -1e30
LOG2_E = 1.4426950408889634
IN_WIDTH = 4 * A_WIDTH + 2 * R_QK_WIDTH + 2 * R_V_WIDTH + 2 * D_MODEL

COL_BLOCK = 512
N_COL_BLOCKS = IN_WIDTH // COL_BLOCK
COL_Q_A, COL_K_A, COL_V_A, COL_G_A, COL_Q_R, COL_K_R = 0, 1, 2, 3, 4, 5
COL_Q_CROSS, COL_K_DECAYED = N_COL_BLOCKS, N_COL_BLOCKS + 1
PROJ_OUT_WIDTH = IN_WIDTH + 2 * COL_BLOCK
WIDE_V_R, WIDE_G_R, WIDE_M_A, WIDE_M_R = 3, 4, 5, 6

LANES = 128
HEADS_PER_LANE_GROUP = LANES // A_HEAD_DIM

PROJ_ROWS = 512
ROW_SPLIT = 2
OUT_ROWS = 1024
OUT_GROUP_ROWS = 256
SEQ_BLOCK = 256
CHUNKS_PER_BLOCK = SEQ_BLOCK // CHUNK
RET_SUB = 4
ATT_SUB = 2
KEY_BLOCKS = N_LEFT_CHUNKS // CHUNKS_PER_BLOCK + 1
VMEM_LIMIT_BYTES = 56 * 1024 * 1024

BF16 = jnp.bfloat16
F32 = jnp.float32


def _resident(shape):
    zeros = (0,) * len(shape)
    return pl.BlockSpec(shape, lambda *_: zeros, pipeline_mode=pl.Buffered(1))


def _sigmoid(x):
    return 0.5 * jnp.tanh(0.5 * x) + 0.5


def _silu(g):
    half = 0.5 * g
    return half * jnp.tanh(half) + half


def _proj_kernel(x_ref, gain_ref, w_ref, cos_ref, sin_ref, cross_ref, sdec_ref, o_ref):
    lane = lax.broadcasted_iota(jnp.int32, (1, COL_BLOCK), 1)
    first_half = (lane % R_KEY_DIM) < (R_KEY_DIM // 2)
    for r in range(ROW_SPLIT):
        rows = slice(r * PROJ_ROWS // ROW_SPLIT, (r + 1) * PROJ_ROWS // ROW_SPLIT)
        x = x_ref[rows, :]
        inv = lax.rsqrt(jnp.mean(x * x, axis=-1, keepdims=True) + NORM_EPS)
        h = (x * inv * gain_ref[...]).astype(BF16)
        for c in range(N_COL_BLOCKS):
            cols = slice(c * COL_BLOCK, (c + 1) * COL_BLOCK)
            acc = jnp.dot(h, w_ref[:, cols], preferred_element_type=F32)
            if c in (COL_Q_R, COL_K_R):
                partner = jnp.where(first_half,
                                    pltpu.roll(acc, COL_BLOCK - R_KEY_DIM // 2, 1),
                                    pltpu.roll(acc, R_KEY_DIM // 2, 1))
                groups = COL_BLOCK // LANES
                cos = jnp.concatenate([cos_ref[rows, :]] * groups, axis=1)
                sin = jnp.concatenate([sin_ref[rows, :]] * groups, axis=1)
                acc = acc * cos + partner * sin
            if c == COL_Q_A:
                acc = acc * (A_HEAD_DIM ** -0.5 * LOG2_E)
            if c == COL_Q_R:
                acc = acc * (R_KEY_DIM ** -0.5)
                o_ref[rows, _block_cols(COL_Q_CROSS)] = (acc * cross_ref[...]).astype(BF16)
            if c == COL_K_R:
                o_ref[rows, _block_cols(COL_K_DECAYED)] = (acc * sdec_ref[...]).astype(BF16)
            o_ref[rows, cols] = acc.astype(BF16)


def _block_cols(c):
    return slice(c * COL_BLOCK, (c + 1) * COL_BLOCK)


def _input_projection(x2d, gain, w_bf16, cos_t, sin_t, cross, sdec, seq_len):
    assert PROJ_ROWS // ROW_SPLIT == SEQ_BLOCK, "a row group is one retention block"
    n_tok = x2d.shape[0]
    seq_tiles = seq_len // PROJ_ROWS
    return pl.pallas_call(
        _proj_kernel,
        grid=(n_tok // PROJ_ROWS,),
        in_specs=[
            pl.BlockSpec((PROJ_ROWS, D_MODEL), lambda i: (i, 0)),
            _resident((1, D_MODEL)),
            _resident((D_MODEL, IN_WIDTH)),
            pl.BlockSpec((PROJ_ROWS, LANES), lambda i: (i % seq_tiles, 0)),
            pl.BlockSpec((PROJ_ROWS, LANES), lambda i: (i % seq_tiles, 0)),
            _resident(cross.shape), _resident(sdec.shape),
        ],
        out_specs=pl.BlockSpec((PROJ_ROWS, PROJ_OUT_WIDTH), lambda i: (i, 0)),
        out_shape=jax.ShapeDtypeStruct((n_tok, PROJ_OUT_WIDTH), BF16),
        compiler_params=pltpu.CompilerParams(
            dimension_semantics=("parallel",), vmem_limit_bytes=VMEM_LIMIT_BYTES),
        name="input_projection",
    )(x2d, gain, w_bf16, cos_t, sin_t, cross, sdec)


def _attn_kernel(q_ref, k_prev_ref, k_cur_ref, v_prev_ref, v_cur_ref, g_ref, bias_ref, o_ref,
                 *s_refs):
    lane = lax.broadcasted_iota(jnp.int32, (1, LANES), 1)
    low_half = lane < A_HEAD_DIM
    halves = (low_half, ~low_half)
    n_pairs = A_HEADS // HEADS_PER_LANE_GROUP
    runtime_zero = jnp.minimum(pl.program_id(1), 0)

    def window(prev_ref, cur_ref, sb, cols):
        blocks = []
        for back in range(KEY_BLOCKS - 1, -1, -1):
            blk = ATT_SUB + sb - back
            ref = prev_ref if blk < ATT_SUB else cur_ref
            start = (blk % ATT_SUB) * SEQ_BLOCK
            blocks.append(ref[start:start + SEQ_BLOCK, cols])
        return jnp.concatenate(blocks, axis=0)

    for sb in range(ATT_SUB):
        rows = slice(sb * SEQ_BLOCK, (sb + 1) * SEQ_BLOCK)
        variant = jnp.minimum(pl.program_id(1) * ATT_SUB + sb, KEY_BLOCKS - 1)

        row_max = []
        for p in range(n_pairs):
            cols = slice(p * LANES, (p + 1) * LANES)
            q_pair = q_ref[rows, cols]
            k_pair = window(k_prev_ref, k_cur_ref, sb, cols)
            for e in range(HEADS_PER_LANE_GROUP):
                h = p * HEADS_PER_LANE_GROUP + e
                q_h = jnp.where(halves[e], q_pair, jnp.zeros_like(q_pair))
                s = lax.dot_general(q_h, k_pair, (((1,), (1,)), ((), ())),
                                    preferred_element_type=F32)
                s = s + bias_ref[variant, h]
                s_refs[sb * A_HEADS + h][0] = s
                row_max.append(jnp.max(s, axis=-1, keepdims=True))

        for p in range(n_pairs):
            cols = slice(p * LANES, (p + 1) * LANES)
            v_pair = window(v_prev_ref, v_cur_ref, sb, cols)
            outs = []
            for e in range(HEADS_PER_LANE_GROUP):
                h = p * HEADS_PER_LANE_GROUP + e
                probs = jnp.exp2(s_refs[sb * A_HEADS + h][runtime_zero] - row_max[h]).astype(BF16)
                v_h = jnp.where(halves[e], v_pair, jnp.ones_like(v_pair))
                acc = jnp.dot(probs, jnp.concatenate([v_h, v_h], axis=1),
                              preferred_element_type=F32)[:, :LANES]
                outs.append(acc / pltpu.roll(acc, A_HEAD_DIM, 1))
            o_pair = jnp.where(low_half, outs[0], outs[1])
            g = g_ref[rows, cols].astype(F32)
            o_ref[rows, cols] = (o_pair * _silu(g)).astype(BF16)


def _band_attention(proj, bias_tile, batch, seq_len):
    assert ATT_SUB >= KEY_BLOCKS - 1, "the key window must fit in the previous and current step"
    n_tok = proj.shape[0]
    step_rows = ATT_SUB * SEQ_BLOCK
    steps_per_seq = seq_len // step_rows

    def rows(b, j):
        return b * steps_per_seq + j

    def cur(col):
        return pl.BlockSpec((step_rows, COL_BLOCK), lambda b, j: (rows(b, j), col))

    def prev(col):
        return pl.BlockSpec((step_rows, COL_BLOCK),
                            lambda b, j: (rows(b, jnp.maximum(j - 1, 0)), col))

    return pl.pallas_call(
        _attn_kernel,
        grid=(batch, steps_per_seq),
        in_specs=[cur(COL_Q_A), prev(COL_K_A), cur(COL_K_A), prev(COL_V_A), cur(COL_V_A),
                  cur(COL_G_A), _resident(bias_tile.shape)],
        out_specs=pl.BlockSpec((step_rows, A_WIDTH), lambda b, j: (rows(b, j), 0)),
        out_shape=jax.ShapeDtypeStruct((n_tok, A_WIDTH), BF16),
        scratch_shapes=[pltpu.VMEM((1, SEQ_BLOCK, KEY_BLOCKS * SEQ_BLOCK), F32)] * (ATT_SUB * A_HEADS),
        compiler_params=pltpu.CompilerParams(
            dimension_semantics=("parallel", "parallel"), vmem_limit_bytes=VMEM_LIMIT_BYTES),
        name="band_attention",
    )(proj, proj, proj, proj, proj, proj, bias_tile)


def _retention_kernel(q_ref, k_ref, qc_ref, kd_ref, v_ref, g_ref, dmat_ref, cdec_ref,
                      gain_ref, o_ref, state_ref):
    @pl.when(pl.program_id(1) == 0)
    def _():
        state_ref[...] = jnp.zeros_like(state_ref)

    n_pairs = R_HEADS // HEADS_PER_LANE_GROUP
    pair_lanes = HEADS_PER_LANE_GROUP * R_VAL_DIM
    low_half = lax.broadcasted_iota(jnp.int32, (1, LANES), 1) < R_KEY_DIM
    even_lanes = lax.broadcasted_iota(jnp.int32, (1, pair_lanes), 1) < R_VAL_DIM
    own_block = ((lax.broadcasted_iota(jnp.int32, (LANES, pair_lanes), 0) < R_KEY_DIM)
                 == (lax.broadcasted_iota(jnp.int32, (LANES, pair_lanes), 1) < R_VAL_DIM))
    states = [state_ref[p] for p in range(n_pairs)]
    for sb in range(RET_SUB):
        rows = slice(sb * SEQ_BLOCK, (sb + 1) * SEQ_BLOCK)
        for p in range(n_pairs):
            cols = slice(p * LANES, (p + 1) * LANES)
            vcols = slice(p * pair_lanes, (p + 1) * pair_lanes)
            q_pair = q_ref[rows, cols]
            k_pair = k_ref[rows, cols]
            v_pair = v_ref[rows, vcols]
            q_cross = qc_ref[rows, cols]
            k_decayed = kd_ref[rows, cols]
            k_zero = jnp.zeros_like(k_pair)
            k_split = jnp.concatenate([jnp.where(low_half, k_pair, k_zero),
                                       jnp.where(low_half, k_zero, k_pair)], axis=0)
            s = lax.dot_general(q_pair, k_split, (((1,), (1,)), ((), ())),
                                preferred_element_type=F32) * dmat_ref[p]
            v_zero = jnp.zeros_like(v_pair)
            v_diag = jnp.concatenate([jnp.where(even_lanes, v_pair, v_zero),
                                      jnp.where(even_lanes, v_zero, v_pair)], axis=0)
            lhs = jnp.concatenate([s.astype(BF16), q_cross], axis=1)
            rhs = jnp.concatenate([v_diag, states[p].astype(BF16)], axis=0)
            o_pair = jnp.dot(lhs, rhs, preferred_element_type=F32)
            kv = lax.dot_general(k_decayed, v_pair, (((0,), (0,)), ((), ())),
                                 preferred_element_type=F32)
            states[p] = states[p] * cdec_ref[p] + jnp.where(own_block, kv, 0.0)
            for e in range(HEADS_PER_LANE_GROUP):
                h = p * HEADS_PER_LANE_GROUP + e
                hcols = slice(h * R_VAL_DIM, (h + 1) * R_VAL_DIM)
                o = o_pair[:, e * R_VAL_DIM:(e + 1) * R_VAL_DIM]
                mean = jnp.mean(o, axis=-1, keepdims=True)
                cen = o - mean
                var = jnp.mean(cen * cen, axis=-1, keepdims=True)
                normed = cen * lax.rsqrt(var + GN_EPS) * gain_ref[:, hcols]
                g = g_ref[rows, hcols].astype(F32)
                o_ref[rows, hcols] = (normed * _silu(g)).astype(BF16)
    for p in range(n_pairs):
        state_ref[p] = states[p]


def _retention(proj, gn_gain, dmat, cdec, batch, seq_len):
    n_tok = proj.shape[0]
    step_rows = RET_SUB * SEQ_BLOCK
    steps_per_seq = seq_len // step_rows
    n_pairs = R_HEADS // HEADS_PER_LANE_GROUP

    def rows(b, j):
        return b * steps_per_seq + j

    return pl.pallas_call(
        _retention_kernel,
        grid=(batch, steps_per_seq),
        in_specs=[
            pl.BlockSpec((step_rows, COL_BLOCK), lambda b, j: (rows(b, j), COL_Q_R)),
            pl.BlockSpec((step_rows, COL_BLOCK), lambda b, j: (rows(b, j), COL_K_R)),
            pl.BlockSpec((step_rows, COL_BLOCK), lambda b, j: (rows(b, j), COL_Q_CROSS)),
            pl.BlockSpec((step_rows, COL_BLOCK), lambda b, j: (rows(b, j), COL_K_DECAYED)),
            pl.BlockSpec((step_rows, R_V_WIDTH), lambda b, j: (rows(b, j), WIDE_V_R)),
            pl.BlockSpec((step_rows, R_V_WIDTH), lambda b, j: (rows(b, j), WIDE_G_R)),
            _resident(dmat.shape), _resident(cdec.shape), _resident(gn_gain.shape),
        ],
        out_specs=pl.BlockSpec((step_rows, R_V_WIDTH), lambda b, j: (rows(b, j), 0)),
        out_shape=jax.ShapeDtypeStruct((n_tok, R_V_WIDTH), BF16),
        scratch_shapes=[pltpu.VMEM((n_pairs, LANES, HEADS_PER_LANE_GROUP * R_VAL_DIM), F32)],
        compiler_params=pltpu.CompilerParams(
            dimension_semantics=("parallel", "arbitrary"), vmem_limit_bytes=VMEM_LIMIT_BYTES),
        name="retention",
    )(proj, proj, proj, proj, proj, proj, dmat, cdec, gn_gain)


def _out_kernel(att_ref, ret_ref, ma_ref, mr_ref, x_ref, wa_ref, wr_ref, wo_ref, fg_ref, o_ref):
    for r in range(OUT_ROWS // OUT_GROUP_ROWS):
        rows = slice(r * OUT_GROUP_ROWS, (r + 1) * OUT_GROUP_ROWS)
        branch_a = jnp.dot(att_ref[rows, :], wa_ref[...], preferred_element_type=F32)
        branch_r = jnp.dot(ret_ref[rows, :], wr_ref[...], preferred_element_type=F32)
        mixed = (_sigmoid(ma_ref[rows, :].astype(F32)) * branch_a
                 + _sigmoid(mr_ref[rows, :].astype(F32)) * branch_r)
        y = x_ref[rows, :] + jnp.dot(mixed.astype(BF16), wo_ref[...], preferred_element_type=F32)
        inv = lax.rsqrt(jnp.mean(y * y, axis=-1, keepdims=True) + NORM_EPS)
        o_ref[rows, :] = y * inv * fg_ref[...]


def _output_block(att, ret, proj, x2d, wa, wr, wo, final_gain):
    n_tok = x2d.shape[0]
    row_spec = lambda width, col: pl.BlockSpec((OUT_ROWS, width), lambda i: (i, col))
    return pl.pallas_call(
        _out_kernel,
        grid=(n_tok // OUT_ROWS,),
        in_specs=[
            row_spec(A_WIDTH, 0), row_spec(R_V_WIDTH, 0),
            row_spec(D_MODEL, WIDE_M_A), row_spec(D_MODEL, WIDE_M_R),
            row_spec(D_MODEL, 0),
            _resident(wa.shape), _resident(wr.shape), _resident(wo.shape),
            _resident(final_gain.shape),
        ],
        out_specs=row_spec(D_MODEL, 0),
        out_shape=jax.ShapeDtypeStruct((n_tok, D_MODEL), F32),
        compiler_params=pltpu.CompilerParams(
            dimension_semantics=("parallel",), vmem_limit_bytes=VMEM_LIMIT_BYTES),
        name="output_block",
    )(att, ret, proj, proj, x2d, wa, wr, wo, final_gain)


def _rotary_tables(seq_len):
    half = R_KEY_DIM // 2
    inv_freq = jnp.power(ROPE_BASE, -jnp.arange(half, dtype=F32) / half)
    ang = jnp.arange(seq_len).astype(F32)[:, None] * inv_freq[None, :]
    cos, sin = jnp.cos(ang), jnp.sin(ang)
    cos_t = jnp.tile(jnp.concatenate([cos, cos], axis=-1), (1, HEADS_PER_LANE_GROUP))
    sin_t = jnp.tile(jnp.concatenate([-sin, sin], axis=-1), (1, HEADS_PER_LANE_GROUP))
    return cos_t, sin_t


def _bias_tile(rel_bias):
    n_keys = KEY_BLOCKS * SEQ_BLOCK
    length = n_keys + SEQ_BLOCK - 1
    below = SEQ_BLOCK - 1 - REL_CLIP
    above = length - below - (2 * REL_CLIP + 1)
    heads = rel_bias.shape[0]
    by_dist = jnp.concatenate([jnp.broadcast_to(rel_bias[:, :1], (heads, below)), rel_bias,
                               jnp.broadcast_to(rel_bias[:, -1:], (heads, above))], axis=1)
    row = jnp.roll(by_dist[:, ::-1], -(SEQ_BLOCK - 1), axis=1)
    tile = jnp.tile(row, (1, SEQ_BLOCK))[:, :SEQ_BLOCK * (length - 1)]
    tile = tile.reshape(heads, SEQ_BLOCK, length - 1)[:, :, :n_keys].astype(F32) * LOG2_E
    qi = np.arange(SEQ_BLOCK)[:, None]
    kk = np.arange(n_keys)[None, :]
    chunk_gap = kk // CHUNK - qi // CHUNK
    in_band = (chunk_gap >= 0) & (chunk_gap <= N_LEFT_CHUNKS)
    variants = [in_band & (kk >= (KEY_BLOCKS - 1 - v) * SEQ_BLOCK) for v in range(KEY_BLOCKS)]
    return jnp.where(np.stack(variants)[:, None], tile[None], NEG_INF)


def _retention_tables():
    n_pairs = R_HEADS // HEADS_PER_LANE_GROUP
    log_gamma = jnp.log1p(-jnp.exp2(-5.0 - jnp.arange(R_HEADS, dtype=F32)))
    pos = jnp.arange(SEQ_BLOCK, dtype=F32)
    diff = pos[:, None] - pos[None, :]
    chunk_diff = (np.arange(SEQ_BLOCK)[:, None] // CHUNK) - (np.arange(SEQ_BLOCK)[None, :] // CHUNK)
    expo = jnp.where(chunk_diff == 0, jnp.abs(diff), diff)
    dmat = jnp.where((chunk_diff >= 0)[None], jnp.exp(expo[None] * log_gamma[:, None, None]), 0.0)
    dmat = dmat.reshape(n_pairs, HEADS_PER_LANE_GROUP, SEQ_BLOCK, SEQ_BLOCK)
    dmat = jnp.concatenate([dmat[:, e] for e in range(HEADS_PER_LANE_GROUP)], axis=-1)
    per_lane = lambda tab, width: jnp.repeat(tab.T, width, axis=1)
    cross = per_lane(jnp.exp((pos + 1.0)[None, :] * log_gamma[:, None]), R_KEY_DIM)
    sdec = per_lane(jnp.exp((SEQ_BLOCK - 1.0 - pos)[None, :] * log_gamma[:, None]), R_KEY_DIM)
    block_decay = jnp.exp(SEQ_BLOCK * log_gamma)
    cdec = jnp.broadcast_to(jnp.repeat(block_decay, R_KEY_DIM).reshape(n_pairs, LANES, 1),
                            (n_pairs, LANES, HEADS_PER_LANE_GROUP * R_VAL_DIM))
    return dmat.astype(F32), cross, sdec, cdec


def kernel(x, norm_gain, w_in, rel_bias, gn_gain, w_out_attn, w_out_ret, w_out, final_gain):
    batch, seq_len, d_model = x.shape
    assert d_model == D_MODEL and norm_gain.shape[0] == 1, "single-layer block of width D_MODEL"
    assert seq_len % PROJ_ROWS == 0 and seq_len % (RET_SUB * SEQ_BLOCK) == 0
    assert seq_len % (ATT_SUB * SEQ_BLOCK) == 0 and seq_len % OUT_ROWS == 0
    x2d = x.reshape(batch * seq_len, D_MODEL)
    cos_t, sin_t = _rotary_tables(seq_len)
    dmat, cross, sdec, cdec = _retention_tables()
    proj = _input_projection(x2d, norm_gain[0][None, :], w_in[0].astype(BF16), cos_t, sin_t,
                             cross, sdec, seq_len)
    att = _band_attention(proj, _bias_tile(rel_bias[0]), batch, seq_len)
    ret = _retention(proj, gn_gain[0][None, :], dmat, cdec, batch, seq_len)
    out = _output_block(att, ret, proj, x2d, w_out_attn[0].astype(BF16), w_out_ret[0].astype(BF16),
                        w_out[0].astype(BF16), final_gain[None, :])
    return out.reshape(batch, seq_len, D_MODEL)
```

```python
import numpy as np
import jax
import jax.numpy as jnp
from jax import lax
from jax.experimental import pallas as pl
from jax.experimental.pallas import tpu as pltpu

D_MODEL = 1024
CHUNK = 64
N_LEFT_CHUNKS = 8
A_HEADS = 8
A_HEAD_DIM = 64
A_WIDTH = A_HEADS * A_HEAD_DIM
REL_CLIP = 128
R_HEADS = 8
R_KEY_DIM = 64
R_VAL_DIM = 128
R_QK_WIDTH = R_HEADS * R_KEY_DIM
R_V_WIDTH = R_HEADS * R_VAL_DIM
ROPE_BASE = 10000.0
NORM_EPS = 1e-6
GN_EPS = 1e-5
NEG_INF = -1e30
LOG2_E = 1.4426950408889634
IN_WIDTH = 4 * A_WIDTH + 2 * R_QK_WIDTH + 2 * R_V_WIDTH + 2 * D_MODEL

COL_BLOCK = 512
N_COL_BLOCKS = IN_WIDTH // COL_BLOCK
COL_Q_A, COL_K_A, COL_V_A, COL_G_A, COL_Q_R, COL_K_R = 0, 1, 2, 3, 4, 5
COL_Q_CROSS, COL_K_DECAYED = N_COL_BLOCKS, N_COL_BLOCKS + 1
PROJ_OUT_WIDTH = IN_WIDTH + 2 * COL_BLOCK
WIDE_V_R, WIDE_G_R, WIDE_M_A, WIDE_M_R = 3, 4, 5, 6

LANES = 128
HEADS_PER_LANE_GROUP = LANES // A_HEAD_DIM

PROJ_ROWS = 512
ROW_SPLIT = 2
OUT_ROWS = 1024
OUT_GROUP_ROWS = 256
SEQ_BLOCK = 256
CHUNKS_PER_BLOCK = SEQ_BLOCK // CHUNK
RET_SUB = 4
ATT_SUB = 2
KEY_BLOCKS = N_LEFT_CHUNKS // CHUNKS_PER_BLOCK + 1
VMEM_LIMIT_BYTES = 56 * 1024 * 1024

BF16 = jnp.bfloat16
F32 = jnp.float32


def _resident(shape):
    zeros = (0,) * len(shape)
    return pl.BlockSpec(shape, lambda *_: zeros, pipeline_mode=pl.Buffered(1))


def _sigmoid(x):
    return 0.5 * jnp.tanh(0.5 * x) + 0.5


def _silu(g):
    half = 0.5 * g
    return half * jnp.tanh(half) + half


def _proj_kernel(x_ref, gain_ref, w_ref, cos_ref, sin_ref, cross_ref, sdec_ref, o_ref):
    lane = lax.broadcasted_iota(jnp.int32, (1, COL_BLOCK), 1)
    first_half = (lane % R_KEY_DIM) < (R_KEY_DIM // 2)
    for r in range(ROW_SPLIT):
        rows = slice(r * PROJ_ROWS // ROW_SPLIT, (r + 1) * PROJ_ROWS // ROW_SPLIT)
        x = x_ref[rows, :]
        inv = lax.rsqrt(jnp.mean(x * x, axis=-1, keepdims=True) + NORM_EPS)
        h = (x * inv * gain_ref[...]).astype(BF16)
        for c in range(N_COL_BLOCKS):
            cols = slice(c * COL_BLOCK, (c + 1) * COL_BLOCK)
            acc = jnp.dot(h, w_ref[:, cols], preferred_element_type=F32)
            if c in (COL_Q_R, COL_K_R):
                partner = jnp.where(first_half,
                                    pltpu.roll(acc, COL_BLOCK - R_KEY_DIM // 2, 1),
                                    pltpu.roll(acc, R_KEY_DIM // 2, 1))
                groups = COL_BLOCK // LANES
                cos = jnp.concatenate([cos_ref[rows, :]] * groups, axis=1)
                sin = jnp.concatenate([sin_ref[rows, :]] * groups, axis=1)
                acc = acc * cos + partner * sin
            if c == COL_Q_A:
                acc = acc * (A_HEAD_DIM ** -0.5 * LOG2_E)
            if c == COL_Q_R:
                acc = acc * (R_KEY_DIM ** -0.5)
                o_ref[rows, _block_cols(COL_Q_CROSS)] = (acc * cross_ref[...]).astype(BF16)
            if c == COL_K_R:
                o_ref[rows, _block_cols(COL_K_DECAYED)] = (acc * sdec_ref[...]).astype(BF16)
            o_ref[rows, cols] = acc.astype(BF16)


def _block_cols(c):
    return slice(c * COL_BLOCK, (c + 1) * COL_BLOCK)


def _input_projection(x2d, gain, w_bf16, cos_t, sin_t, cross, sdec, seq_len):
    assert PROJ_ROWS // ROW_SPLIT == SEQ_BLOCK, "a row group is one retention block"
    n_tok = x2d.shape[0]
    seq_tiles = seq_len // PROJ_ROWS
    return pl.pallas_call(
        _proj_kernel,
        grid=(n_tok // PROJ_ROWS,),
        in_specs=[
            pl.BlockSpec((PROJ_ROWS, D_MODEL), lambda i: (i, 0)),
            _resident((1, D_MODEL)),
            _resident((D_MODEL, IN_WIDTH)),
            pl.BlockSpec((PROJ_ROWS, LANES), lambda i: (i % seq_tiles, 0)),
            pl.BlockSpec((PROJ_ROWS, LANES), lambda i: (i % seq_tiles, 0)),
            _resident(cross.shape), _resident(sdec.shape),
        ],
        out_specs=pl.BlockSpec((PROJ_ROWS, PROJ_OUT_WIDTH), lambda i: (i, 0)),
        out_shape=jax.ShapeDtypeStruct((n_tok, PROJ_OUT_WIDTH), BF16),
        compiler_params=pltpu.CompilerParams(
            dimension_semantics=("parallel",), vmem_limit_bytes=VMEM_LIMIT_BYTES),
        name="input_projection",
    )(x2d, gain, w_bf16, cos_t, sin_t, cross, sdec)


def _attn_kernel(q_ref, k_prev_ref, k_cur_ref, v_prev_ref, v_cur_ref, g_ref, bias_ref, o_ref,
                 *s_refs):
    lane = lax.broadcasted_iota(jnp.int32, (1, LANES), 1)
    low_half = lane < A_HEAD_DIM
    halves = (low_half, ~low_half)
    n_pairs = A_HEADS // HEADS_PER_LANE_GROUP
    runtime_zero = jnp.minimum(pl.program_id(1), 0)

    def window(prev_ref, cur_ref, sb, cols):
        blocks = []
        for back in range(KEY_BLOCKS - 1, -1, -1):
            blk = ATT_SUB + sb - back
            ref = prev_ref if blk < ATT_SUB else cur_ref
            start = (blk % ATT_SUB) * SEQ_BLOCK
            blocks.append(ref[start:start + SEQ_BLOCK, cols])
        return jnp.concatenate(blocks, axis=0)

    row_half = (lax.broadcasted_iota(jnp.int32, (LANES, 1), 0) < A_HEAD_DIM)
    row_halves = (row_half, ~row_half)

    col_max = []
    for sb in range(ATT_SUB):
        rows = slice(sb * SEQ_BLOCK, (sb + 1) * SEQ_BLOCK)
        variant = jnp.minimum(pl.program_id(1) * ATT_SUB + sb, KEY_BLOCKS - 1)

        for p in range(n_pairs):
            cols = slice(p * LANES, (p + 1) * LANES)
            q_pair = q_ref[rows, cols]
            k_pair = window(k_prev_ref, k_cur_ref, sb, cols)
            for e in range(HEADS_PER_LANE_GROUP):
                h = p * HEADS_PER_LANE_GROUP + e
                q_h = jnp.where(halves[e], q_pair, jnp.zeros_like(q_pair))
                s = lax.dot_general(k_pair, q_h, (((1,), (1,)), ((), ())),
                                    preferred_element_type=F32)
                s = s + bias_ref[variant, h]
                s_refs[sb * A_HEADS + h][0] = s
                col_max.append(jnp.max(s, axis=0, keepdims=True))

    for sb in range(ATT_SUB):
        rows = slice(sb * SEQ_BLOCK, (sb + 1) * SEQ_BLOCK)
        for p in range(n_pairs):
            cols = slice(p * LANES, (p + 1) * LANES)
            v_t = window(v_prev_ref, v_cur_ref, sb, cols).T
            outs = []
            for e in range(HEADS_PER_LANE_GROUP):
                h = p * HEADS_PER_LANE_GROUP + e
                probs = jnp.exp2(s_refs[sb * A_HEADS + h][runtime_zero]
                                 - col_max[sb * A_HEADS + h]).astype(BF16)
                v_h = jnp.where(row_halves[e], v_t, jnp.ones_like(v_t))
                acc = jnp.dot(v_h, probs, preferred_element_type=F32)
                own = slice(e * A_HEAD_DIM, (e + 1) * A_HEAD_DIM)
                other = slice((1 - e) * A_HEAD_DIM, (2 - e) * A_HEAD_DIM)
                outs.append(acc[own] / acc[other])
            o_pair = jnp.concatenate(outs, axis=0).T
            g = g_ref[rows, cols].astype(F32)
            o_ref[rows, cols] = (o_pair * _silu(g)).astype(BF16)


def _band_attention(proj, bias_tile, batch, seq_len):
    assert ATT_SUB >= KEY_BLOCKS - 1, "the key window must fit in the previous and current step"
    n_tok = proj.shape[0]
    step_rows = ATT_SUB * SEQ_BLOCK
    steps_per_seq = seq_len // step_rows

    def rows(b, j):
        return b * steps_per_seq + j

    def cur(col):
        return pl.BlockSpec((step_rows, COL_BLOCK), lambda b, j: (rows(b, j), col))

    def prev(col):
        return pl.BlockSpec((step_rows, COL_BLOCK),
                            lambda b, j: (rows(b, jnp.maximum(j - 1, 0)), col))

    return pl.pallas_call(
        _attn_kernel,
        grid=(batch, steps_per_seq),
        in_specs=[cur(COL_Q_A), prev(COL_K_A), cur(COL_K_A), prev(COL_V_A), cur(COL_V_A),
                  cur(COL_G_A), _resident(bias_tile.shape)],
        out_specs=pl.BlockSpec((step_rows, A_WIDTH), lambda b, j: (rows(b, j), 0)),
        out_shape=jax.ShapeDtypeStruct((n_tok, A_WIDTH), BF16),
        scratch_shapes=[pltpu.VMEM((1, KEY_BLOCKS * SEQ_BLOCK, SEQ_BLOCK), F32)] * (ATT_SUB * A_HEADS),
        compiler_params=pltpu.CompilerParams(
            dimension_semantics=("parallel", "parallel"), vmem_limit_bytes=VMEM_LIMIT_BYTES),
        name="band_attention",
    )(proj, proj, proj, proj, proj, proj, bias_tile)


def _retention_kernel(q_ref, k_ref, qc_ref, kd_ref, v_ref, g_ref, dmat_ref, cdec_ref,
                      gain_ref, o_ref, state_ref):
    @pl.when(pl.program_id(1) == 0)
    def _():
        state_ref[...] = jnp.zeros_like(state_ref)

    n_pairs = R_HEADS // HEADS_PER_LANE_GROUP
    pair_lanes = HEADS_PER_LANE_GROUP * R_VAL_DIM
    low_half = lax.broadcasted_iota(jnp.int32, (1, LANES), 1) < R_KEY_DIM
    even_lanes = lax.broadcasted_iota(jnp.int32, (1, pair_lanes), 1) < R_VAL_DIM
    own_block = ((lax.broadcasted_iota(jnp.int32, (LANES, pair_lanes), 0) < R_KEY_DIM)
                 == (lax.broadcasted_iota(jnp.int32, (LANES, pair_lanes), 1) < R_VAL_DIM))
    states = [state_ref[p] for p in range(n_pairs)]
    for sb in range(RET_SUB):
        rows = slice(sb * SEQ_BLOCK, (sb + 1) * SEQ_BLOCK)
        for p in range(n_pairs):
            cols = slice(p * LANES, (p + 1) * LANES)
            vcols = slice(p * pair_lanes, (p + 1) * pair_lanes)
            q_pair = q_ref[rows, cols]
            k_pair = k_ref[rows, cols]
            v_pair = v_ref[rows, vcols]
            q_cross = qc_ref[rows, cols]
            k_decayed = kd_ref[rows, cols]
            k_zero = jnp.zeros_like(k_pair)
            k_split = jnp.concatenate([jnp.where(low_half, k_pair, k_zero),
                                       jnp.where(low_half, k_zero, k_pair)], axis=0)
            s = lax.dot_general(q_pair, k_split, (((1,), (1,)), ((), ())),
                                preferred_element_type=F32) * dmat_ref[p]
            v_zero = jnp.zeros_like(v_pair)
            v_diag = jnp.concatenate([jnp.where(even_lanes, v_pair, v_zero),
                                      jnp.where(even_lanes, v_zero, v_pair)], axis=0)
            lhs = jnp.concatenate([s.astype(BF16), q_cross], axis=1)
            rhs = jnp.concatenate([v_diag, states[p].astype(BF16)], axis=0)
            o_pair = jnp.dot(lhs, rhs, preferred_element_type=F32)
            kv = lax.dot_general(k_decayed, v_pair, (((0,), (0,)), ((), ())),
                                 preferred_element_type=F32)
            states[p] = states[p] * cdec_ref[p] + jnp.where(own_block, kv, 0.0)
            for e in range(HEADS_PER_LANE_GROUP):
                h = p * HEADS_PER_LANE_GROUP + e
                hcols = slice(h * R_VAL_DIM, (h + 1) * R_VAL_DIM)
                o = o_pair[:, e * R_VAL_DIM:(e + 1) * R_VAL_DIM]
                mean = jnp.mean(o, axis=-1, keepdims=True)
                cen = o - mean
                var = jnp.mean(cen * cen, axis=-1, keepdims=True)
                normed = cen * lax.rsqrt(var + GN_EPS) * gain_ref[:, hcols]
                g = g_ref[rows, hcols].astype(F32)
                o_ref[rows, hcols] = (normed * _silu(g)).astype(BF16)
    for p in range(n_pairs):
        state_ref[p] = states[p]


def _retention(proj, gn_gain, dmat, cdec, batch, seq_len):
    n_tok = proj.shape[0]
    step_rows = RET_SUB * SEQ_BLOCK
    steps_per_seq = seq_len // step_rows
    n_pairs = R_HEADS // HEADS_PER_LANE_GROUP

    def rows(b, j):
        return b * steps_per_seq + j

    return pl.pallas_call(
        _retention_kernel,
        grid=(batch, steps_per_seq),
        in_specs=[
            pl.BlockSpec((step_rows, COL_BLOCK), lambda b, j: (rows(b, j), COL_Q_R)),
            pl.BlockSpec((step_rows, COL_BLOCK), lambda b, j: (rows(b, j), COL_K_R)),
            pl.BlockSpec((step_rows, COL_BLOCK), lambda b, j: (rows(b, j), COL_Q_CROSS)),
            pl.BlockSpec((step_rows, COL_BLOCK), lambda b, j: (rows(b, j), COL_K_DECAYED)),
            pl.BlockSpec((step_rows, R_V_WIDTH), lambda b, j: (rows(b, j), WIDE_V_R)),
            pl.BlockSpec((step_rows, R_V_WIDTH), lambda b, j: (rows(b, j), WIDE_G_R)),
            _resident(dmat.shape), _resident(cdec.shape), _resident(gn_gain.shape),
        ],
        out_specs=pl.BlockSpec((step_rows, R_V_WIDTH), lambda b, j: (rows(b, j), 0)),
        out_shape=jax.ShapeDtypeStruct((n_tok, R_V_WIDTH), BF16),
        scratch_shapes=[pltpu.VMEM((n_pairs, LANES, HEADS_PER_LANE_GROUP * R_VAL_DIM), F32)],
        compiler_params=pltpu.CompilerParams(
            dimension_semantics=("parallel", "arbitrary"), vmem_limit_bytes=VMEM_LIMIT_BYTES),
        name="retention",
    )(proj, proj, proj, proj, proj, proj, dmat, cdec, gn_gain)


def _out_kernel(att_ref, ret_ref, ma_ref, mr_ref, x_ref, wa_ref, wr_ref, wo_ref, fg_ref, o_ref):
    for r in range(OUT_ROWS // OUT_GROUP_ROWS):
        rows = slice(r * OUT_GROUP_ROWS, (r + 1) * OUT_GROUP_ROWS)
        branch_a = jnp.dot(att_ref[rows, :], wa_ref[...], preferred_element_type=F32)
        branch_r = jnp.dot(ret_ref[rows, :], wr_ref[...], preferred_element_type=F32)
        mixed = (_sigmoid(ma_ref[rows, :].astype(F32)) * branch_a
                 + _sigmoid(mr_ref[rows, :].astype(F32)) * branch_r)
        y = x_ref[rows, :] + jnp.dot(mixed.astype(BF16), wo_ref[...], preferred_element_type=F32)
        inv = lax.rsqrt(jnp.mean(y * y, axis=-1, keepdims=True) + NORM_EPS)
        o_ref[rows, :] = y * inv * fg_ref[...]


def _output_block(att, ret, proj, x2d, wa, wr, wo, final_gain):
    n_tok = x2d.shape[0]
    row_spec = lambda width, col: pl.BlockSpec((OUT_ROWS, width), lambda i: (i, col))
    return pl.pallas_call(
        _out_kernel,
        grid=(n_tok // OUT_ROWS,),
        in_specs=[
            row_spec(A_WIDTH, 0), row_spec(R_V_WIDTH, 0),
            row_spec(D_MODEL, WIDE_M_A), row_spec(D_MODEL, WIDE_M_R),
            row_spec(D_MODEL, 0),
            _resident(wa.shape), _resident(wr.shape), _resident(wo.shape),
            _resident(final_gain.shape),
        ],
        out_specs=row_spec(D_MODEL, 0),
        out_shape=jax.ShapeDtypeStruct((n_tok, D_MODEL), F32),
        compiler_params=pltpu.CompilerParams(
            dimension_semantics=("parallel",), vmem_limit_bytes=VMEM_LIMIT_BYTES),
        name="output_block",
    )(att, ret, proj, proj, x2d, wa, wr, wo, final_gain)


def _rotary_tables(seq_len):
    half = R_KEY_DIM // 2
    inv_freq = jnp.power(ROPE_BASE, -jnp.arange(half, dtype=F32) / half)
    ang = jnp.arange(seq_len).astype(F32)[:, None] * inv_freq[None, :]
    cos, sin = jnp.cos(ang), jnp.sin(ang)
    cos_t = jnp.tile(jnp.concatenate([cos, cos], axis=-1), (1, HEADS_PER_LANE_GROUP))
    sin_t = jnp.tile(jnp.concatenate([-sin, sin], axis=-1), (1, HEADS_PER_LANE_GROUP))
    return cos_t, sin_t


def _bias_tile(rel_bias):
    n_keys = KEY_BLOCKS * SEQ_BLOCK
    length = n_keys + SEQ_BLOCK - 1
    below = SEQ_BLOCK - 1 - REL_CLIP
    above = length - below - (2 * REL_CLIP + 1)
    heads = rel_bias.shape[0]
    by_dist = jnp.concatenate([jnp.broadcast_to(rel_bias[:, :1], (heads, below)), rel_bias,
                               jnp.broadcast_to(rel_bias[:, -1:], (heads, above))], axis=1)
    row = jnp.roll(by_dist[:, ::-1], -(SEQ_BLOCK - 1), axis=1)
    tile = jnp.tile(row, (1, SEQ_BLOCK))[:, :SEQ_BLOCK * (length - 1)]
    tile = tile.reshape(heads, SEQ_BLOCK, length - 1)[:, :, :n_keys].astype(F32) * LOG2_E
    qi = np.arange(SEQ_BLOCK)[:, None]
    kk = np.arange(n_keys)[None, :]
    chunk_gap = kk // CHUNK - qi // CHUNK
    in_band = (chunk_gap >= 0) & (chunk_gap <= N_LEFT_CHUNKS)
    variants = [in_band & (kk >= (KEY_BLOCKS - 1 - v) * SEQ_BLOCK) for v in range(KEY_BLOCKS)]
    tiles = jnp.where(np.stack(variants)[:, None], tile[None], NEG_INF)
    return jnp.swapaxes(tiles, -1, -2)


def _retention_tables():
    n_pairs = R_HEADS // HEADS_PER_LANE_GROUP
    log_gamma = jnp.log1p(-jnp.exp2(-5.0 - jnp.arange(R_HEADS, dtype=F32)))
    pos = jnp.arange(SEQ_BLOCK, dtype=F32)
    diff = pos[:, None] - pos[None, :]
    chunk_diff = (np.arange(SEQ_BLOCK)[:, None] // CHUNK) - (np.arange(SEQ_BLOCK)[None, :] // CHUNK)
    expo = jnp.where(chunk_diff == 0, jnp.abs(diff), diff)
    dmat = jnp.where((chunk_diff >= 0)[None], jnp.exp(expo[None] * log_gamma[:, None, None]), 0.0)
    dmat = dmat.reshape(n_pairs, HEADS_PER_LANE_GROUP, SEQ_BLOCK, SEQ_BLOCK)
    dmat = jnp.concatenate([dmat[:, e] for e in range(HEADS_PER_LANE_GROUP)], axis=-1)
    per_lane = lambda tab, width: jnp.repeat(tab.T, width, axis=1)
    cross = per_lane(jnp.exp((pos + 1.0)[None, :] * log_gamma[:, None]), R_KEY_DIM)
    sdec = per_lane(jnp.exp((SEQ_BLOCK - 1.0 - pos)[None, :] * log_gamma[:, None]), R_KEY_DIM)
    block_decay = jnp.exp(SEQ_BLOCK * log_gamma)
    cdec = jnp.broadcast_to(jnp.repeat(block_decay, R_KEY_DIM).reshape(n_pairs, LANES, 1),
                            (n_pairs, LANES, HEADS_PER_LANE_GROUP * R_VAL_DIM))
    return dmat.astype(F32), cross, sdec, cdec


def kernel(x, norm_gain, w_in, rel_bias, gn_gain, w_out_attn, w_out_ret, w_out, final_gain):
    batch, seq_len, d_model = x.shape
    assert d_model == D_MODEL and norm_gain.shape[0] == 1, "single-layer block of width D_MODEL"
    assert seq_len % PROJ_ROWS == 0 and seq_len % (RET_SUB * SEQ_BLOCK) == 0
    assert seq_len % (ATT_SUB * SEQ_BLOCK) == 0 and seq_len % OUT_ROWS == 0
    x2d = x.reshape(batch * seq_len, D_MODEL)
    cos_t, sin_t = _rotary_tables(seq_len)
    dmat, cross, sdec, cdec = _retention_tables()
    proj = _input_projection(x2d, norm_gain[0][None, :], w_in[0].astype(BF16), cos_t, sin_t,
                             cross, sdec, seq_len)
    att = _band_attention(proj, _bias_tile(rel_bias[0]), batch, seq_len)
    ret = _retention(proj, gn_gain[0][None, :], dmat, cdec, batch, seq_len)
    out = _output_block(att, ret, proj, x2d, w_out_attn[0].astype(BF16), w_out_ret[0].astype(BF16),
                        w_out[0].astype(BF16), final_gain[None, :])
    return out.reshape(batch, seq_len, D_MODEL)
```

```python
import numpy as np
import jax
import jax.numpy as jnp
from jax import lax
from jax.experimental import pallas as pl
from jax.experimental.pallas import tpu as pltpu

D_MODEL = 1024
CHUNK = 64
N_LEFT_CHUNKS = 8
A_HEADS = 8
A_HEAD_DIM = 64
A_WIDTH = A_HEADS * A_HEAD_DIM
REL_CLIP = 128
R_HEADS = 8
R_KEY_DIM = 64
R_VAL_DIM = 128
R_QK_WIDTH = R_HEADS * R_KEY_DIM
R_V_WIDTH = R_HEADS * R_VAL_DIM
ROPE_BASE = 10000.0
NORM_EPS = 1e-6
GN_EPS = 1e-5
NEG_INF = -1e30
LOG2_E = 1.4426950408889634
IN_WIDTH = 4 * A_WIDTH + 2 * R_QK_WIDTH + 2 * R_V_WIDTH + 2 * D_MODEL

COL_BLOCK = 512
N_COL_BLOCKS = IN_WIDTH // COL_BLOCK
COL_Q_A, COL_K_A, COL_V_A, COL_G_A, COL_Q_R, COL_K_R = 0, 1, 2, 3, 4, 5
COL_Q_CROSS, COL_K_DECAYED = N_COL_BLOCKS, N_COL_BLOCKS + 1
PROJ_OUT_WIDTH = IN_WIDTH + 2 * COL_BLOCK
WIDE_V_R, WIDE_G_R, WIDE_M_A, WIDE_M_R = 3, 4, 5, 6

LANES = 128
SUBLANES = 8
HEADS_PER_LANE_GROUP = LANES // A_HEAD_DIM

PROJ_ROWS = 512
ROW_SPLIT = 2
OUT_ROWS = 1024
OUT_GROUP_ROWS = 256
SEQ_BLOCK = 256
CHUNKS_PER_BLOCK = SEQ_BLOCK // CHUNK
RET_SUB = 4
ATT_SUB = 2
KEY_BLOCKS = N_LEFT_CHUNKS // CHUNKS_PER_BLOCK + 1
BIAS_RING = (KEY_BLOCKS + 1) * SEQ_BLOCK
VMEM_LIMIT_BYTES = 56 * 1024 * 1024

BF16 = jnp.bfloat16
F32 = jnp.float32


def _resident(shape):
    zeros = (0,) * len(shape)
    return pl.BlockSpec(shape, lambda *_: zeros, pipeline_mode=pl.Buffered(1))


def _sigmoid(x):
    return 0.5 * jnp.tanh(0.5 * x) + 0.5


def _silu(g):
    half = 0.5 * g
    return half * jnp.tanh(half) + half


def _proj_kernel(x_ref, gain_ref, w_ref, cos_ref, sin_ref, cross_ref, sdec_ref, o_ref):
    lane = lax.broadcasted_iota(jnp.int32, (1, COL_BLOCK), 1)
    first_half = (lane % R_KEY_DIM) < (R_KEY_DIM // 2)
    for r in range(ROW_SPLIT):
        rows = slice(r * PROJ_ROWS // ROW_SPLIT, (r + 1) * PROJ_ROWS // ROW_SPLIT)
        x = x_ref[rows, :]
        inv = lax.rsqrt(jnp.mean(x * x, axis=-1, keepdims=True) + NORM_EPS)
        h = (x * inv * gain_ref[...]).astype(BF16)
        for c in range(N_COL_BLOCKS):
            cols = slice(c * COL_BLOCK, (c + 1) * COL_BLOCK)
            acc = jnp.dot(h, w_ref[:, cols], preferred_element_type=F32)
            if c in (COL_Q_R, COL_K_R):
                partner = jnp.where(first_half,
                                    pltpu.roll(acc, COL_BLOCK - R_KEY_DIM // 2, 1),
                                    pltpu.roll(acc, R_KEY_DIM // 2, 1))
                groups = COL_BLOCK // LANES
                cos = jnp.concatenate([cos_ref[rows, :]] * groups, axis=1)
                sin = jnp.concatenate([sin_ref[rows, :]] * groups, axis=1)
                acc = acc * cos + partner * sin
            if c == COL_Q_A:
                acc = acc * (A_HEAD_DIM ** -0.5 * LOG2_E)
            if c == COL_Q_R:
                acc = acc * (R_KEY_DIM ** -0.5)
                o_ref[rows, _block_cols(COL_Q_CROSS)] = (acc * cross_ref[...]).astype(BF16)
            if c == COL_K_R:
                o_ref[rows, _block_cols(COL_K_DECAYED)] = (acc * sdec_ref[...]).astype(BF16)
            o_ref[rows, cols] = acc.astype(BF16)


def _block_cols(c):
    return slice(c * COL_BLOCK, (c + 1) * COL_BLOCK)


def _input_projection(x2d, gain, w_bf16, cos_t, sin_t, cross, sdec, seq_len):
    assert PROJ_ROWS // ROW_SPLIT == SEQ_BLOCK, "a row group is one retention block"
    n_tok = x2d.shape[0]
    seq_tiles = seq_len // PROJ_ROWS
    return pl.pallas_call(
        _proj_kernel,
        grid=(n_tok // PROJ_ROWS,),
        in_specs=[
            pl.BlockSpec((PROJ_ROWS, D_MODEL), lambda i: (i, 0)),
            _resident((1, D_MODEL)),
            _resident((D_MODEL, IN_WIDTH)),
            pl.BlockSpec((PROJ_ROWS, LANES), lambda i: (i % seq_tiles, 0)),
            pl.BlockSpec((PROJ_ROWS, LANES), lambda i: (i % seq_tiles, 0)),
            _resident(cross.shape), _resident(sdec.shape),
        ],
        out_specs=pl.BlockSpec((PROJ_ROWS, PROJ_OUT_WIDTH), lambda i: (i, 0)),
        out_shape=jax.ShapeDtypeStruct((n_tok, PROJ_OUT_WIDTH), BF16),
        compiler_params=pltpu.CompilerParams(
            dimension_semantics=("parallel",), vmem_limit_bytes=VMEM_LIMIT_BYTES),
        name="input_projection",
    )(x2d, gain, w_bf16, cos_t, sin_t, cross, sdec)


def _build_bias_tiles(ring_ref, bias_ref):
    n_keys = KEY_BLOCKS * SEQ_BLOCK
    sublane = lax.broadcasted_iota(jnp.int32, (SUBLANES, BIAS_RING), 0)
    lane = lax.broadcasted_iota(jnp.int32, (SUBLANES, SEQ_BLOCK), 1)
    masked = jnp.full((SUBLANES, SEQ_BLOCK), NEG_INF, F32)
    for h in range(A_HEADS):
        base = ring_ref[h]
        shift = 1
        while shift < SUBLANES:
            base = jnp.where((sublane & shift) != 0, pltpu.roll(base, shift, 1), base)
            shift *= 2
        for a in range(n_keys // SUBLANES):
            key_chunk = a * SUBLANES // CHUNK
            rows = slice(a * SUBLANES, (a + 1) * SUBLANES)
            tile = pltpu.roll(base, a * SUBLANES, 1)[:, :SEQ_BLOCK]
            in_band = ((lane >= (key_chunk - N_LEFT_CHUNKS) * CHUNK)
                       & (lane < (key_chunk + 1) * CHUNK))
            tile = jnp.where(in_band, tile, NEG_INF)
            for v in range(KEY_BLOCKS):
                in_sequence = a * SUBLANES >= (KEY_BLOCKS - 1 - v) * SEQ_BLOCK
                bias_ref[v, h, rows, :] = tile if in_sequence else masked


def _attn_kernel(q_ref, k_prev_ref, k_cur_ref, v_prev_ref, v_cur_ref, g_ref, ring_ref, o_ref,
                 bias_ref, *s_refs):
    @pl.when((pl.program_id(0) == 0) & (pl.program_id(1) == 0))
    def _():
        _build_bias_tiles(ring_ref, bias_ref)

    lane = lax.broadcasted_iota(jnp.int32, (1, LANES), 1)
    low_half = lane < A_HEAD_DIM
    halves = (low_half, ~low_half)
    n_pairs = A_HEADS // HEADS_PER_LANE_GROUP
    runtime_zero = jnp.minimum(pl.program_id(1), 0)

    def window(prev_ref, cur_ref, sb, cols):
        blocks = []
        for back in range(KEY_BLOCKS - 1, -1, -1):
            blk = ATT_SUB + sb - back
            ref = prev_ref if blk < ATT_SUB else cur_ref
            start = (blk % ATT_SUB) * SEQ_BLOCK
            blocks.append(ref[start:start + SEQ_BLOCK, cols])
        return jnp.concatenate(blocks, axis=0)

    row_half = (lax.broadcasted_iota(jnp.int32, (LANES, 1), 0) < A_HEAD_DIM)
    row_halves = (row_half, ~row_half)

    col_max = []
    for sb in range(ATT_SUB):
        rows = slice(sb * SEQ_BLOCK, (sb + 1) * SEQ_BLOCK)
        variant = jnp.minimum(pl.program_id(1) * ATT_SUB + sb, KEY_BLOCKS - 1)

        for p in range(n_pairs):
            cols = slice(p * LANES, (p + 1) * LANES)
            q_pair = q_ref[rows, cols]
            k_pair = window(k_prev_ref, k_cur_ref, sb, cols)
            for e in range(HEADS_PER_LANE_GROUP):
                h = p * HEADS_PER_LANE_GROUP + e
                q_h = jnp.where(halves[e], q_pair, jnp.zeros_like(q_pair))
                s = lax.dot_general(k_pair, q_h, (((1,), (1,)), ((), ())),
                                    preferred_element_type=F32)
                s = s + bias_ref[variant, h]
                s_refs[sb * A_HEADS + h][0] = s
                col_max.append(jnp.max(s, axis=0, keepdims=True))

    for sb in range(ATT_SUB):
        rows = slice(sb * SEQ_BLOCK, (sb + 1) * SEQ_BLOCK)
        for p in range(n_pairs):
            cols = slice(p * LANES, (p + 1) * LANES)
            v_t = window(v_prev_ref, v_cur_ref, sb, cols).T
            outs = []
            for e in range(HEADS_PER_LANE_GROUP):
                h = p * HEADS_PER_LANE_GROUP + e
                probs = jnp.exp2(s_refs[sb * A_HEADS + h][runtime_zero]
                                 - col_max[sb * A_HEADS + h]).astype(BF16)
                v_h = jnp.where(row_halves[e], v_t, jnp.ones_like(v_t))
                acc = jnp.dot(v_h, probs, preferred_element_type=F32)
                own = slice(e * A_HEAD_DIM, (e + 1) * A_HEAD_DIM)
                other = slice((1 - e) * A_HEAD_DIM, (2 - e) * A_HEAD_DIM)
                outs.append(acc[own] / acc[other])
            o_pair = jnp.concatenate(outs, axis=0).T
            g = g_ref[rows, cols].astype(F32)
            o_ref[rows, cols] = (o_pair * _silu(g)).astype(BF16)


def _band_attention(proj, bias_ring, batch, seq_len):
    assert ATT_SUB >= KEY_BLOCKS - 1, "the key window must fit in the previous and current step"
    n_keys = KEY_BLOCKS * SEQ_BLOCK
    n_tok = proj.shape[0]
    step_rows = ATT_SUB * SEQ_BLOCK
    steps_per_seq = seq_len // step_rows

    def rows(b, j):
        return b * steps_per_seq + j

    def cur(col):
        return pl.BlockSpec((step_rows, COL_BLOCK), lambda b, j: (rows(b, j), col))

    def prev(col):
        return pl.BlockSpec((step_rows, COL_BLOCK),
                            lambda b, j: (rows(b, jnp.maximum(j - 1, 0)), col))

    return pl.pallas_call(
        _attn_kernel,
        grid=(batch, steps_per_seq),
        in_specs=[cur(COL_Q_A), prev(COL_K_A), cur(COL_K_A), prev(COL_V_A), cur(COL_V_A),
                  cur(COL_G_A), _resident(bias_ring.shape)],
        out_specs=pl.BlockSpec((step_rows, A_WIDTH), lambda b, j: (rows(b, j), 0)),
        out_shape=jax.ShapeDtypeStruct((n_tok, A_WIDTH), BF16),
        scratch_shapes=[pltpu.VMEM((KEY_BLOCKS, A_HEADS, n_keys, SEQ_BLOCK), F32)]
        + [pltpu.VMEM((1, n_keys, SEQ_BLOCK), F32)] * (ATT_SUB * A_HEADS),
        compiler_params=pltpu.CompilerParams(
            dimension_semantics=("arbitrary", "arbitrary"), vmem_limit_bytes=VMEM_LIMIT_BYTES),
        name="band_attention",
    )(proj, proj, proj, proj, proj, proj, bias_ring)


def _retention_kernel(q_ref, k_ref, qc_ref, kd_ref, v_ref, g_ref, dmat_ref, cdec_ref,
                      gain_ref, o_ref, state_ref):
    @pl.when(pl.program_id(1) == 0)
    def _():
        state_ref[...] = jnp.zeros_like(state_ref)

    n_pairs = R_HEADS // HEADS_PER_LANE_GROUP
    pair_lanes = HEADS_PER_LANE_GROUP * R_VAL_DIM
    low_half = lax.broadcasted_iota(jnp.int32, (1, LANES), 1) < R_KEY_DIM
    even_lanes = lax.broadcasted_iota(jnp.int32, (1, pair_lanes), 1) < R_VAL_DIM
    own_block = ((lax.broadcasted_iota(jnp.int32, (LANES, pair_lanes), 0) < R_KEY_DIM)
                 == (lax.broadcasted_iota(jnp.int32, (LANES, pair_lanes), 1) < R_VAL_DIM))
    states = [state_ref[p] for p in range(n_pairs)]
    for sb in range(RET_SUB):
        rows = slice(sb * SEQ_BLOCK, (sb + 1) * SEQ_BLOCK)
        for p in range(n_pairs):
            cols = slice(p * LANES, (p + 1) * LANES)
            vcols = slice(p * pair_lanes, (p + 1) * pair_lanes)
            q_pair = q_ref[rows, cols]
            k_pair = k_ref[rows, cols]
            v_pair = v_ref[rows, vcols]
            q_cross = qc_ref[rows, cols]
            k_decayed = kd_ref[rows, cols]
            k_zero = jnp.zeros_like(k_pair)
            k_split = jnp.concatenate([jnp.where(low_half, k_pair, k_zero),
                                       jnp.where(low_half, k_zero, k_pair)], axis=0)
            s = lax.dot_general(q_pair, k_split, (((1,), (1,)), ((), ())),
                                preferred_element_type=F32) * dmat_ref[p]
            v_zero = jnp.zeros_like(v_pair)
            v_diag = jnp.concatenate([jnp.where(even_lanes, v_pair, v_zero),
                                      jnp.where(even_lanes, v_zero, v_pair)], axis=0)
            lhs = jnp.concatenate([s.astype(BF16), q_cross], axis=1)
            rhs = jnp.concatenate([v_diag, states[p].astype(BF16)], axis=0)
            o_pair = jnp.dot(lhs, rhs, preferred_element_type=F32)
            kv = lax.dot_general(k_decayed, v_pair, (((0,), (0,)), ((), ())),
                                 preferred_element_type=F32)
            states[p] = states[p] * cdec_ref[p] + jnp.where(own_block, kv, 0.0)
            for e in range(HEADS_PER_LANE_GROUP):
                h = p * HEADS_PER_LANE_GROUP + e
                hcols = slice(h * R_VAL_DIM, (h + 1) * R_VAL_DIM)
                o = o_pair[:, e * R_VAL_DIM:(e + 1) * R_VAL_DIM]
                mean = jnp.mean(o, axis=-1, keepdims=True)
                cen = o - mean
                var = jnp.mean(cen * cen, axis=-1, keepdims=True)
                normed = cen * lax.rsqrt(var + GN_EPS) * gain_ref[:, hcols]
                g = g_ref[rows, hcols].astype(F32)
                o_ref[rows, hcols] = (normed * _silu(g)).astype(BF16)
    for p in range(n_pairs):
        state_ref[p] = states[p]


def _retention(proj, gn_gain, dmat, cdec, batch, seq_len):
    n_tok = proj.shape[0]
    step_rows = RET_SUB * SEQ_BLOCK
    steps_per_seq = seq_len // step_rows
    n_pairs = R_HEADS // HEADS_PER_LANE_GROUP

    def rows(b, j):
        return b * steps_per_seq + j

    return pl.pallas_call(
        _retention_kernel,
        grid=(batch, steps_per_seq),
        in_specs=[
            pl.BlockSpec((step_rows, COL_BLOCK), lambda b, j: (rows(b, j), COL_Q_R)),
            pl.BlockSpec((step_rows, COL_BLOCK), lambda b, j: (rows(b, j), COL_K_R)),
            pl.BlockSpec((step_rows, COL_BLOCK), lambda b, j: (rows(b, j), COL_Q_CROSS)),
            pl.BlockSpec((step_rows, COL_BLOCK), lambda b, j: (rows(b, j), COL_K_DECAYED)),
            pl.BlockSpec((step_rows, R_V_WIDTH), lambda b, j: (rows(b, j), WIDE_V_R)),
            pl.BlockSpec((step_rows, R_V_WIDTH), lambda b, j: (rows(b, j), WIDE_G_R)),
            _resident(dmat.shape), _resident(cdec.shape), _resident(gn_gain.shape),
        ],
        out_specs=pl.BlockSpec((step_rows, R_V_WIDTH), lambda b, j: (rows(b, j), 0)),
        out_shape=jax.ShapeDtypeStruct((n_tok, R_V_WIDTH), BF16),
        scratch_shapes=[pltpu.VMEM((n_pairs, LANES, HEADS_PER_LANE_GROUP * R_VAL_DIM), F32)],
        compiler_params=pltpu.CompilerParams(
            dimension_semantics=("parallel", "arbitrary"), vmem_limit_bytes=VMEM_LIMIT_BYTES),
        name="retention",
    )(proj, proj, proj, proj, proj, proj, dmat, cdec, gn_gain)


def _out_kernel(att_ref, ret_ref, ma_ref, mr_ref, x_ref, wa_ref, wr_ref, wo_ref, fg_ref, o_ref):
    for r in range(OUT_ROWS // OUT_GROUP_ROWS):
        rows = slice(r * OUT_GROUP_ROWS, (r + 1) * OUT_GROUP_ROWS)
        branch_a = jnp.dot(att_ref[rows, :], wa_ref[...], preferred_element_type=F32)
        branch_r = jnp.dot(ret_ref[rows, :], wr_ref[...], preferred_element_type=F32)
        mixed = (_sigmoid(ma_ref[rows, :].astype(F32)) * branch_a
                 + _sigmoid(mr_ref[rows, :].astype(F32)) * branch_r)
        y = x_ref[rows, :] + jnp.dot(mixed.astype(BF16), wo_ref[...], preferred_element_type=F32)
        inv = lax.rsqrt(jnp.mean(y * y, axis=-1, keepdims=True) + NORM_EPS)
        o_ref[rows, :] = y * inv * fg_ref[...]


def _output_block(att, ret, proj, x2d, wa, wr, wo, final_gain):
    n_tok = x2d.shape[0]
    row_spec = lambda width, col: pl.BlockSpec((OUT_ROWS, width), lambda i: (i, col))
    return pl.pallas_call(
        _out_kernel,
        grid=(n_tok // OUT_ROWS,),
        in_specs=[
            row_spec(A_WIDTH, 0), row_spec(R_V_WIDTH, 0),
            row_spec(D_MODEL, WIDE_M_A), row_spec(D_MODEL, WIDE_M_R),
            row_spec(D_MODEL, 0),
            _resident(wa.shape), _resident(wr.shape), _resident(wo.shape),
            _resident(final_gain.shape),
        ],
        out_specs=row_spec(D_MODEL, 0),
        out_shape=jax.ShapeDtypeStruct((n_tok, D_MODEL), F32),
        compiler_params=pltpu.CompilerParams(
            dimension_semantics=("parallel",), vmem_limit_bytes=VMEM_LIMIT_BYTES),
        name="output_block",
    )(att, ret, proj, proj, x2d, wa, wr, wo, final_gain)


def _rotary_tables(seq_len):
    half = R_KEY_DIM // 2
    inv_freq = jnp.power(ROPE_BASE, -jnp.arange(half, dtype=F32) / half)
    ang = jnp.arange(seq_len).astype(F32)[:, None] * inv_freq[None, :]
    cos, sin = jnp.cos(ang), jnp.sin(ang)
    cos_t = jnp.tile(jnp.concatenate([cos, cos], axis=-1), (1, HEADS_PER_LANE_GROUP))
    sin_t = jnp.tile(jnp.concatenate([-sin, sin], axis=-1), (1, HEADS_PER_LANE_GROUP))
    return cos_t, sin_t


def _take_columns(table, idx):
    pieces, start = [], 0
    while start < len(idx):
        end = start + 1
        if end < len(idx) and idx[end] == idx[start]:
            while end < len(idx) and idx[end] == idx[start]:
                end += 1
            piece = jnp.broadcast_to(table[:, idx[start]:idx[start] + 1], (table.shape[0], end - start))
        else:
            while end < len(idx) and idx[end] == idx[end - 1] + 1:
                end += 1
            piece = table[:, idx[start]:idx[end - 1] + 1]
        pieces.append(piece)
        start = end
    return jnp.concatenate(pieces, axis=1)


def _bias_ring(rel_bias):
    n_keys = KEY_BLOCKS * SEQ_BLOCK
    assert n_keys + SEQ_BLOCK - 1 <= BIAS_RING
    u = np.arange(BIAS_RING)
    t = np.where(u < SEQ_BLOCK, u, u - BIAS_RING)
    idx = np.clip((KEY_BLOCKS - 1) * SEQ_BLOCK + t, -REL_CLIP, REL_CLIP) + REL_CLIP
    ring = _take_columns(rel_bias.astype(F32), idx) * LOG2_E
    return jnp.broadcast_to(ring[:, None, :], (rel_bias.shape[0], SUBLANES, BIAS_RING))


def _retention_tables():
    n_pairs = R_HEADS // HEADS_PER_LANE_GROUP
    log_gamma = jnp.log1p(-jnp.exp2(-5.0 - jnp.arange(R_HEADS, dtype=F32)))
    pos = jnp.arange(SEQ_BLOCK, dtype=F32)
    diff = pos[:, None] - pos[None, :]
    chunk_diff = (np.arange(SEQ_BLOCK)[:, None] // CHUNK) - (np.arange(SEQ_BLOCK)[None, :] // CHUNK)
    expo = jnp.where(chunk_diff == 0, jnp.abs(diff), diff)
    dmat = jnp.where((chunk_diff >= 0)[None], jnp.exp(expo[None] * log_gamma[:, None, None]), 0.0)
    dmat = dmat.reshape(n_pairs, HEADS_PER_LANE_GROUP, SEQ_BLOCK, SEQ_BLOCK)
    dmat = jnp.concatenate([dmat[:, e] for e in range(HEADS_PER_LANE_GROUP)], axis=-1)
    per_lane = lambda tab, width: jnp.repeat(tab.T, width, axis=1)
    cross = per_lane(jnp.exp((pos + 1.0)[None, :] * log_gamma[:, None]), R_KEY_DIM)
    sdec = per_lane(jnp.exp((SEQ_BLOCK - 1.0 - pos)[None, :] * log_gamma[:, None]), R_KEY_DIM)
    block_decay = jnp.exp(SEQ_BLOCK * log_gamma)
    cdec = jnp.broadcast_to(jnp.repeat(block_decay, R_KEY_DIM).reshape(n_pairs, LANES, 1),
                            (n_pairs, LANES, HEADS_PER_LANE_GROUP * R_VAL_DIM))
    return dmat.astype(F32), cross, sdec, cdec


def kernel(x, norm_gain, w_in, rel_bias, gn_gain, w_out_attn, w_out_ret, w_out, final_gain):
    batch, seq_len, d_model = x.shape
    assert d_model == D_MODEL and norm_gain.shape[0] == 1, "single-layer block of width D_MODEL"
    assert seq_len % PROJ_ROWS == 0 and seq_len % (RET_SUB * SEQ_BLOCK) == 0
    assert seq_len % (ATT_SUB * SEQ_BLOCK) == 0 and seq_len % OUT_ROWS == 0
    x2d = x.reshape(batch * seq_len, D_MODEL)
    cos_t, sin_t = _rotary_tables(seq_len)
    dmat, cross, sdec, cdec = _retention_tables()
    proj = _input_projection(x2d, norm_gain[0][None, :], w_in[0].astype(BF16), cos_t, sin_t,
                             cross, sdec, seq_len)
    att = _band_attention(proj, _bias_ring(rel_bias[0]), batch, seq_len)
    ret = _retention(proj, gn_gain[0][None, :], dmat, cdec, batch, seq_len)
    out = _output_block(att, ret, proj, x2d, w_out_attn[0].astype(BF16), w_out_ret[0].astype(BF16),
                        w_out[0].astype(BF16), final_gain[None, :])
    return out.reshape(batch, seq_len, D_MODEL)
```

```python
import functools

import numpy as np
import jax
import jax.numpy as jnp
from jax import lax
from jax.experimental import pallas as pl
from jax.experimental.pallas import tpu as pltpu

D_MODEL = 1024
CHUNK = 64
N_LEFT_CHUNKS = 8
A_HEADS = 8
A_HEAD_DIM = 64
A_WIDTH = A_HEADS * A_HEAD_DIM
REL_CLIP = 128
R_HEADS = 8
R_KEY_DIM = 64
R_VAL_DIM = 128
R_QK_WIDTH = R_HEADS * R_KEY_DIM
R_V_WIDTH = R_HEADS * R_VAL_DIM
ROPE_BASE = 10000.0
NORM_EPS = 1e-6
GN_EPS = 1e-5
NEG_INF = -1e30
LOG2_E = 1.4426950408889634
IN_WIDTH = 4 * A_WIDTH + 2 * R_QK_WIDTH + 2 * R_V_WIDTH + 2 * D_MODEL

COL_BLOCK = 512
N_COL_BLOCKS = IN_WIDTH // COL_BLOCK
W_Q_A, W_K_A, W_V_A, W_G_A, W_Q_R, W_K_R = 0, 1, 2, 3, 4, 5
W_V_R, W_G_R, W_M_A, W_M_R = (6, 7), (8, 9), (10, 11), (12, 13)
RETENTION_W_BLOCKS = (W_Q_R, W_K_R) + W_V_R + W_G_R
COL_Q_A, COL_K_A, COL_V_A, COL_G_A = 0, 1, 2, 3
STORED_BLOCK = {W_Q_A: COL_Q_A, W_K_A: COL_K_A, W_V_A: COL_V_A, W_G_A: COL_G_A,
                W_M_A[0]: 4, W_M_A[1]: 5, W_M_R[0]: 6, W_M_R[1]: 7}
PROJ_OUT_WIDTH = len(STORED_BLOCK) * COL_BLOCK
WIDE_M_A, WIDE_M_R = 2, 3

LANES = 128
SUBLANES = 8
HEADS_PER_LANE_GROUP = LANES // A_HEAD_DIM

PROJ_ROWS = 512
OUT_ROWS = 1024
OUT_GROUP_ROWS = 256
SEQ_BLOCK = 256
CHUNKS_PER_BLOCK = SEQ_BLOCK // CHUNK
ATT_SUB = 2
KEY_BLOCKS = N_LEFT_CHUNKS // CHUNKS_PER_BLOCK + 1
BIAS_RING = (KEY_BLOCKS + 1) * SEQ_BLOCK
VMEM_LIMIT_BYTES = 56 * 1024 * 1024

BF16 = jnp.bfloat16
F32 = jnp.float32


def _resident(shape):
    zeros = (0,) * len(shape)
    return pl.BlockSpec(shape, lambda *_: zeros, pipeline_mode=pl.Buffered(1))


def _sigmoid(x):
    return 0.5 * jnp.tanh(0.5 * x) + 0.5


def _silu(g):
    half = 0.5 * g
    return half * jnp.tanh(half) + half


def _retention_block(r, rows, q_scr, k_scr, qc_scr, kd_scr, v_scr, g_scr, dmat_ref, cdec_ref,
                     gain_ref, o_ref, states):
    pair_lanes = HEADS_PER_LANE_GROUP * R_VAL_DIM
    low_half = lax.broadcasted_iota(jnp.int32, (1, LANES), 1) < R_KEY_DIM
    even_lanes = lax.broadcasted_iota(jnp.int32, (1, pair_lanes), 1) < R_VAL_DIM
    own_block = ((lax.broadcasted_iota(jnp.int32, (LANES, pair_lanes), 0) < R_KEY_DIM)
                 == (lax.broadcasted_iota(jnp.int32, (LANES, pair_lanes), 1) < R_VAL_DIM))
    new_states = []
    for p in range(R_HEADS // HEADS_PER_LANE_GROUP):
        cols = slice(p * LANES, (p + 1) * LANES)
        vcols = slice(p * pair_lanes, (p + 1) * pair_lanes)
        q_pair = q_scr[r, :, cols]
        k_pair = k_scr[r, :, cols]
        v_pair = v_scr[r, :, vcols]
        k_zero = jnp.zeros_like(k_pair)
        k_split = jnp.concatenate([jnp.where(low_half, k_pair, k_zero),
                                   jnp.where(low_half, k_zero, k_pair)], axis=0)
        s = lax.dot_general(q_pair, k_split, (((1,), (1,)), ((), ())),
                            preferred_element_type=F32) * dmat_ref[p]
        v_zero = jnp.zeros_like(v_pair)
        v_diag = jnp.concatenate([jnp.where(even_lanes, v_pair, v_zero),
                                  jnp.where(even_lanes, v_zero, v_pair)], axis=0)
        lhs = jnp.concatenate([s.astype(BF16), qc_scr[r, :, cols]], axis=1)
        rhs = jnp.concatenate([v_diag, states[p].astype(BF16)], axis=0)
        o_pair = jnp.dot(lhs, rhs, preferred_element_type=F32)
        kv = lax.dot_general(kd_scr[r, :, cols], v_pair, (((0,), (0,)), ((), ())),
                             preferred_element_type=F32)
        new_states.append(states[p] * cdec_ref[p] + jnp.where(own_block, kv, 0.0))
        for e in range(HEADS_PER_LANE_GROUP):
            h = p * HEADS_PER_LANE_GROUP + e
            hcols = slice(h * R_VAL_DIM, (h + 1) * R_VAL_DIM)
            o = o_pair[:, e * R_VAL_DIM:(e + 1) * R_VAL_DIM]
            mean = jnp.mean(o, axis=-1, keepdims=True)
            cen = o - mean
            var = jnp.mean(cen * cen, axis=-1, keepdims=True)
            normed = cen * lax.rsqrt(var + GN_EPS) * gain_ref[:, hcols]
            g = g_scr[r, :, hcols].astype(F32)
            o_ref[rows, hcols] = (normed * _silu(g)).astype(BF16)
    return new_states


def _proj_kernel(steps_per_seq, x_ref, gain_ref, w_ref, cos_ref, sin_ref, cross_ref, sdec_ref,
                 dmat_ref, cdec_ref, gn_gain_ref, o_ref, ret_ref, state_ref,
                 q_scr, k_scr, qc_scr, kd_scr, v_scr, g_scr):
    @pl.when(pl.program_id(0) % steps_per_seq == 0)
    def _():
        state_ref[...] = jnp.zeros_like(state_ref)

    lane = lax.broadcasted_iota(jnp.int32, (1, COL_BLOCK), 1)
    first_half = (lane % R_KEY_DIM) < (R_KEY_DIM // 2)
    groups = COL_BLOCK // LANES
    n_pairs = R_HEADS // HEADS_PER_LANE_GROUP
    states = [state_ref[p] for p in range(n_pairs)]
    for r in range(PROJ_ROWS // SEQ_BLOCK):
        rows = slice(r * SEQ_BLOCK, (r + 1) * SEQ_BLOCK)
        x = x_ref[rows, :]
        inv = lax.rsqrt(jnp.mean(x * x, axis=-1, keepdims=True) + NORM_EPS)
        h = (x * inv * gain_ref[...]).astype(BF16)

        for c in RETENTION_W_BLOCKS:
            acc = jnp.dot(h, w_ref[:, _block_cols(c)], preferred_element_type=F32)
            if c in (W_Q_R, W_K_R):
                partner = jnp.where(first_half,
                                    pltpu.roll(acc, COL_BLOCK - R_KEY_DIM // 2, 1),
                                    pltpu.roll(acc, R_KEY_DIM // 2, 1))
                cos = jnp.concatenate([cos_ref[rows, :]] * groups, axis=1)
                sin = jnp.concatenate([sin_ref[rows, :]] * groups, axis=1)
                acc = acc * cos + partner * sin
            if c == W_Q_R:
                acc = acc * (R_KEY_DIM ** -0.5)
                q_scr[r] = acc.astype(BF16)
                qc_scr[r] = (acc * cross_ref[...]).astype(BF16)
            elif c == W_K_R:
                k_scr[r] = acc.astype(BF16)
                kd_scr[r] = (acc * sdec_ref[...]).astype(BF16)
            elif c in W_V_R:
                v_scr[r, :, _block_cols(W_V_R.index(c))] = acc.astype(BF16)
            else:
                g_scr[r, :, _block_cols(W_G_R.index(c))] = acc.astype(BF16)
        states = _retention_block(r, rows, q_scr, k_scr, qc_scr, kd_scr, v_scr, g_scr, dmat_ref,
                                  cdec_ref, gn_gain_ref, ret_ref, states)

        for c, stored in STORED_BLOCK.items():
            acc = jnp.dot(h, w_ref[:, _block_cols(c)], preferred_element_type=F32)
            if c == W_Q_A:
                acc = acc * (A_HEAD_DIM ** -0.5 * LOG2_E)
            o_ref[rows, _block_cols(stored)] = acc.astype(BF16)
    for p in range(n_pairs):
        state_ref[p] = states[p]


def _block_cols(c):
    return slice(c * COL_BLOCK, (c + 1) * COL_BLOCK)


def _projection_and_retention(x2d, gain, w_bf16, cos_t, sin_t, tables, gn_gain, seq_len):
    dmat, cross, sdec, cdec = tables
    n_tok = x2d.shape[0]
    steps_per_seq = seq_len // PROJ_ROWS
    row_groups = PROJ_ROWS // SEQ_BLOCK
    n_pairs = R_HEADS // HEADS_PER_LANE_GROUP
    narrow = pltpu.VMEM((row_groups, SEQ_BLOCK, R_QK_WIDTH), BF16)
    wide = pltpu.VMEM((row_groups, SEQ_BLOCK, R_V_WIDTH), BF16)
    return pl.pallas_call(
        functools.partial(_proj_kernel, steps_per_seq),
        grid=(n_tok // PROJ_ROWS,),
        in_specs=[
            pl.BlockSpec((PROJ_ROWS, D_MODEL), lambda i: (i, 0)),
            _resident((1, D_MODEL)),
            _resident((D_MODEL, IN_WIDTH)),
            pl.BlockSpec((PROJ_ROWS, LANES), lambda i: (i % steps_per_seq, 0)),
            pl.BlockSpec((PROJ_ROWS, LANES), lambda i: (i % steps_per_seq, 0)),
            _resident(cross.shape), _resident(sdec.shape), _resident(dmat.shape),
            _resident(cdec.shape), _resident(gn_gain.shape),
        ],
        out_specs=[pl.BlockSpec((PROJ_ROWS, PROJ_OUT_WIDTH), lambda i: (i, 0)),
                   pl.BlockSpec((PROJ_ROWS, R_V_WIDTH), lambda i: (i, 0))],
        out_shape=[jax.ShapeDtypeStruct((n_tok, PROJ_OUT_WIDTH), BF16),
                   jax.ShapeDtypeStruct((n_tok, R_V_WIDTH), BF16)],
        scratch_shapes=[pltpu.VMEM((n_pairs, LANES, HEADS_PER_LANE_GROUP * R_VAL_DIM), F32),
                        narrow, narrow, narrow, narrow, wide, wide],
        compiler_params=pltpu.CompilerParams(
            dimension_semantics=("arbitrary",), vmem_limit_bytes=VMEM_LIMIT_BYTES),
        name="projection_retention",
    )(x2d, gain, w_bf16, cos_t, sin_t, cross, sdec, dmat, cdec, gn_gain)


def _build_bias_tiles(ring_ref, bias_ref):
    n_keys = KEY_BLOCKS * SEQ_BLOCK
    sublane = lax.broadcasted_iota(jnp.int32, (SUBLANES, BIAS_RING), 0)
    lane = lax.broadcasted_iota(jnp.int32, (SUBLANES, SEQ_BLOCK), 1)
    masked = jnp.full((SUBLANES, SEQ_BLOCK), NEG_INF, F32)
    for h in range(A_HEADS):
        base = ring_ref[h]
        shift = 1
        while shift < SUBLANES:
            base = jnp.where((sublane & shift) != 0, pltpu.roll(base, shift, 1), base)
            shift *= 2
        for a in range(n_keys // SUBLANES):
            key_chunk = a * SUBLANES // CHUNK
            rows = slice(a * SUBLANES, (a + 1) * SUBLANES)
            tile = pltpu.roll(base, a * SUBLANES, 1)[:, :SEQ_BLOCK]
            in_band = ((lane >= (key_chunk - N_LEFT_CHUNKS) * CHUNK)
                       & (lane < (key_chunk + 1) * CHUNK))
            tile = jnp.where(in_band, tile, NEG_INF)
            for v in range(KEY_BLOCKS):
                in_sequence = a * SUBLANES >= (KEY_BLOCKS - 1 - v) * SEQ_BLOCK
                bias_ref[v, h, rows, :] = tile if in_sequence else masked


def _attn_kernel(q_ref, k_prev_ref, k_cur_ref, v_prev_ref, v_cur_ref, g_ref, ring_ref, o_ref,
                 bias_ref, *s_refs):
    @pl.when((pl.program_id(0) == 0) & (pl.program_id(1) == 0))
    def _():
        _build_bias_tiles(ring_ref, bias_ref)

    lane = lax.broadcasted_iota(jnp.int32, (1, LANES), 1)
    low_half = lane < A_HEAD_DIM
    halves = (low_half, ~low_half)
    n_pairs = A_HEADS // HEADS_PER_LANE_GROUP
    runtime_zero = jnp.minimum(pl.program_id(1), 0)

    def window(prev_ref, cur_ref, sb, cols):
        blocks = []
        for back in range(KEY_BLOCKS - 1, -1, -1):
            blk = ATT_SUB + sb - back
            ref = prev_ref if blk < ATT_SUB else cur_ref
            start = (blk % ATT_SUB) * SEQ_BLOCK
            blocks.append(ref[start:start + SEQ_BLOCK, cols])
        return jnp.concatenate(blocks, axis=0)

    row_half = (lax.broadcasted_iota(jnp.int32, (LANES, 1), 0) < A_HEAD_DIM)
    row_halves = (row_half, ~row_half)

    col_max = []
    for sb in range(ATT_SUB):
        rows = slice(sb * SEQ_BLOCK, (sb + 1) * SEQ_BLOCK)
        variant = jnp.minimum(pl.program_id(1) * ATT_SUB + sb, KEY_BLOCKS - 1)

        for p in range(n_pairs):
            cols = slice(p * LANES, (p + 1) * LANES)
            q_pair = q_ref[rows, cols]
            k_pair = window(k_prev_ref, k_cur_ref, sb, cols)
            for e in range(HEADS_PER_LANE_GROUP):
                h = p * HEADS_PER_LANE_GROUP + e
                q_h = jnp.where(halves[e], q_pair, jnp.zeros_like(q_pair))
                s = lax.dot_general(k_pair, q_h, (((1,), (1,)), ((), ())),
                                    preferred_element_type=F32)
                s = s + bias_ref[variant, h]
                s_refs[sb * A_HEADS + h][0] = s
                col_max.append(jnp.max(s, axis=0, keepdims=True))

    for sb in range(ATT_SUB):
        rows = slice(sb * SEQ_BLOCK, (sb + 1) * SEQ_BLOCK)
        for p in range(n_pairs):
            cols = slice(p * LANES, (p + 1) * LANES)
            v_t = window(v_prev_ref, v_cur_ref, sb, cols).T
            outs = []
            for e in range(HEADS_PER_LANE_GROUP):
                h = p * HEADS_PER_LANE_GROUP + e
                probs = jnp.exp2(s_refs[sb * A_HEADS + h][runtime_zero]
                                 - col_max[sb * A_HEADS + h]).astype(BF16)
                v_h = jnp.where(row_halves[e], v_t, jnp.ones_like(v_t))
                acc = jnp.dot(v_h, probs, preferred_element_type=F32)
                own = slice(e * A_HEAD_DIM, (e + 1) * A_HEAD_DIM)
                other = slice((1 - e) * A_HEAD_DIM, (2 - e) * A_HEAD_DIM)
                outs.append(acc[own] / acc[other])
            o_pair = jnp.concatenate(outs, axis=0).T
            g = g_ref[rows, cols].astype(F32)
            o_ref[rows, cols] = (o_pair * _silu(g)).astype(BF16)


def _band_attention(proj, bias_ring, batch, seq_len):
    assert ATT_SUB >= KEY_BLOCKS - 1, "the key window must fit in the previous and current step"
    n_keys = KEY_BLOCKS * SEQ_BLOCK
    n_tok = proj.shape[0]
    step_rows = ATT_SUB * SEQ_BLOCK
    steps_per_seq = seq_len // step_rows

    def rows(b, j):
        return b * steps_per_seq + j

    def cur(col):
        return pl.BlockSpec((step_rows, COL_BLOCK), lambda b, j: (rows(b, j), col))

    def prev(col):
        return pl.BlockSpec((step_rows, COL_BLOCK),
                            lambda b, j: (rows(b, jnp.maximum(j - 1, 0)), col))

    return pl.pallas_call(
        _attn_kernel,
        grid=(batch, steps_per_seq),
        in_specs=[cur(COL_Q_A), prev(COL_K_A), cur(COL_K_A), prev(COL_V_A), cur(COL_V_A),
                  cur(COL_G_A), _resident(bias_ring.shape)],
        out_specs=pl.BlockSpec((step_rows, A_WIDTH), lambda b, j: (rows(b, j), 0)),
        out_shape=jax.ShapeDtypeStruct((n_tok, A_WIDTH), BF16),
        scratch_shapes=[pltpu.VMEM((KEY_BLOCKS, A_HEADS, n_keys, SEQ_BLOCK), F32)]
        + [pltpu.VMEM((1, n_keys, SEQ_BLOCK), F32)] * (ATT_SUB * A_HEADS),
        compiler_params=pltpu.CompilerParams(
            dimension_semantics=("arbitrary", "arbitrary"), vmem_limit_bytes=VMEM_LIMIT_BYTES),
        name="band_attention",
    )(proj, proj, proj, proj, proj, proj, bias_ring)


def _out_kernel(att_ref, ret_ref, ma_ref, mr_ref, x_ref, wa_ref, wr_ref, wo_ref, fg_ref, o_ref):
    for r in range(OUT_ROWS // OUT_GROUP_ROWS):
        rows = slice(r * OUT_GROUP_ROWS, (r + 1) * OUT_GROUP_ROWS)
        branch_a = jnp.dot(att_ref[rows, :], wa_ref[...], preferred_element_type=F32)
        branch_r = jnp.dot(ret_ref[rows, :], wr_ref[...], preferred_element_type=F32)
        mixed = (_sigmoid(ma_ref[rows, :].astype(F32)) * branch_a
                 + _sigmoid(mr_ref[rows, :].astype(F32)) * branch_r)
        y = x_ref[rows, :] + jnp.dot(mixed.astype(BF16), wo_ref[...], preferred_element_type=F32)
        inv = lax.rsqrt(jnp.mean(y * y, axis=-1, keepdims=True) + NORM_EPS)
        o_ref[rows, :] = y * inv * fg_ref[...]


def _output_block(att, ret, proj, x2d, wa, wr, wo, final_gain):
    n_tok = x2d.shape[0]
    row_spec = lambda width, col: pl.BlockSpec((OUT_ROWS, width), lambda i: (i, col))
    return pl.pallas_call(
        _out_kernel,
        grid=(n_tok // OUT_ROWS,),
        in_specs=[
            row_spec(A_WIDTH, 0), row_spec(R_V_WIDTH, 0),
            row_spec(D_MODEL, WIDE_M_A), row_spec(D_MODEL, WIDE_M_R),
            row_spec(D_MODEL, 0),
            _resident(wa.shape), _resident(wr.shape), _resident(wo.shape),
            _resident(final_gain.shape),
        ],
        out_specs=row_spec(D_MODEL, 0),
        out_shape=jax.ShapeDtypeStruct((n_tok, D_MODEL), F32),
        compiler_params=pltpu.CompilerParams(
            dimension_semantics=("parallel",), vmem_limit_bytes=VMEM_LIMIT_BYTES),
        name="output_block",
    )(att, ret, proj, proj, x2d, wa, wr, wo, final_gain)


def _rotary_tables(seq_len):
    half = R_KEY_DIM // 2
    inv_freq = jnp.power(ROPE_BASE, -jnp.arange(half, dtype=F32) / half)
    ang = jnp.arange(seq_len).astype(F32)[:, None] * inv_freq[None, :]
    cos, sin = jnp.cos(ang), jnp.sin(ang)
    cos_t = jnp.tile(jnp.concatenate([cos, cos], axis=-1), (1, HEADS_PER_LANE_GROUP))
    sin_t = jnp.tile(jnp.concatenate([-sin, sin], axis=-1), (1, HEADS_PER_LANE_GROUP))
    return cos_t, sin_t


def _take_columns(table, idx):
    pieces, start = [], 0
    while start < len(idx):
        end = start + 1
        if end < len(idx) and idx[end] == idx[start]:
            while end < len(idx) and idx[end] == idx[start]:
                end += 1
            piece = jnp.broadcast_to(table[:, idx[start]:idx[start] + 1], (table.shape[0], end - start))
        else:
            while end < len(idx) and idx[end] == idx[end - 1] + 1:
                end += 1
            piece = table[:, idx[start]:idx[end - 1] + 1]
        pieces.append(piece)
        start = end
    return jnp.concatenate(pieces, axis=1)


def _bias_ring(rel_bias):
    n_keys = KEY_BLOCKS * SEQ_BLOCK
    assert n_keys + SEQ_BLOCK - 1 <= BIAS_RING
    u = np.arange(BIAS_RING)
    t = np.where(u < SEQ_BLOCK, u, u - BIAS_RING)
    idx = np.clip((KEY_BLOCKS - 1) * SEQ_BLOCK + t, -REL_CLIP, REL_CLIP) + REL_CLIP
    ring = _take_columns(rel_bias.astype(F32), idx) * LOG2_E
    return jnp.broadcast_to(ring[:, None, :], (rel_bias.shape[0], SUBLANES, BIAS_RING))


def _retention_tables():
    n_pairs = R_HEADS // HEADS_PER_LANE_GROUP
    log_gamma = jnp.log1p(-jnp.exp2(-5.0 - jnp.arange(R_HEADS, dtype=F32)))
    pos = jnp.arange(SEQ_BLOCK, dtype=F32)
    diff = pos[:, None] - pos[None, :]
    chunk_diff = (np.arange(SEQ_BLOCK)[:, None] // CHUNK) - (np.arange(SEQ_BLOCK)[None, :] // CHUNK)
    expo = jnp.where(chunk_diff == 0, jnp.abs(diff), diff)
    dmat = jnp.where((chunk_diff >= 0)[None], jnp.exp(expo[None] * log_gamma[:, None, None]), 0.0)
    dmat = dmat.reshape(n_pairs, HEADS_PER_LANE_GROUP, SEQ_BLOCK, SEQ_BLOCK)
    dmat = jnp.concatenate([dmat[:, e] for e in range(HEADS_PER_LANE_GROUP)], axis=-1)
    per_lane = lambda tab, width: jnp.repeat(tab.T, width, axis=1)
    cross = per_lane(jnp.exp((pos + 1.0)[None, :] * log_gamma[:, None]), R_KEY_DIM)
    sdec = per_lane(jnp.exp((SEQ_BLOCK - 1.0 - pos)[None, :] * log_gamma[:, None]), R_KEY_DIM)
    block_decay = jnp.exp(SEQ_BLOCK * log_gamma)
    cdec = jnp.broadcast_to(jnp.repeat(block_decay, R_KEY_DIM).reshape(n_pairs, LANES, 1),
                            (n_pairs, LANES, HEADS_PER_LANE_GROUP * R_VAL_DIM))
    return dmat.astype(F32), cross, sdec, cdec


def kernel(x, norm_gain, w_in, rel_bias, gn_gain, w_out_attn, w_out_ret, w_out, final_gain):
    batch, seq_len, d_model = x.shape
    assert d_model == D_MODEL and norm_gain.shape[0] == 1, "single-layer block of width D_MODEL"
    assert seq_len % PROJ_ROWS == 0 and PROJ_ROWS % SEQ_BLOCK == 0
    assert seq_len % (ATT_SUB * SEQ_BLOCK) == 0 and seq_len % OUT_ROWS == 0
    x2d = x.reshape(batch * seq_len, D_MODEL)
    cos_t, sin_t = _rotary_tables(seq_len)
    proj, ret = _projection_and_retention(x2d, norm_gain[0][None, :], w_in[0].astype(BF16), cos_t, sin_t,
                                          _retention_tables(), gn_gain[0][None, :], seq_len)
    att = _band_attention(proj, _bias_ring(rel_bias[0]), batch, seq_len)
    out = _output_block(att, ret, proj, x2d, w_out_attn[0].astype(BF16), w_out_ret[0].astype(BF16),
                        w_out[0].astype(BF16), final_gain[None, :])
    return out.reshape(batch, seq_len, D_MODEL)
```

```python
import functools

import numpy as np
import jax
import jax.numpy as jnp
from jax import lax
from jax.experimental import pallas as pl
from jax.experimental.pallas import tpu as pltpu

D_MODEL = 1024
CHUNK = 64
N_LEFT_CHUNKS = 8
A_HEADS = 8
A_HEAD_DIM = 64
A_WIDTH = A_HEADS * A_HEAD_DIM
REL_CLIP = 128
R_HEADS = 8
R_KEY_DIM = 64
R_VAL_DIM = 128
R_QK_WIDTH = R_HEADS * R_KEY_DIM
R_V_WIDTH = R_HEADS * R_VAL_DIM
ROPE_BASE = 10000.0
NORM_EPS = 1e-6
GN_EPS = 1e-5
NEG_INF = -1e30
LOG2_E = 1.4426950408889634
IN_WIDTH = 4 * A_WIDTH + 2 * R_QK_WIDTH + 2 * R_V_WIDTH + 2 * D_MODEL

COL_BLOCK = 512
N_COL_BLOCKS = IN_WIDTH // COL_BLOCK
W_Q_A, W_K_A, W_V_A, W_G_A, W_Q_R, W_K_R = 0, 1, 2, 3, 4, 5
W_V_R, W_G_R, W_M_A, W_M_R = (6, 7), (8, 9), (10, 11), (12, 13)
RETENTION_W_BLOCKS = (W_Q_R, W_K_R) + W_V_R + W_G_R
COL_Q_A, COL_K_A, COL_V_A, COL_G_A = 0, 1, 2, 3
STORED_BLOCK = {W_Q_A: COL_Q_A, W_K_A: COL_K_A, W_V_A: COL_V_A, W_G_A: COL_G_A,
                W_M_A[0]: 4, W_M_A[1]: 5, W_M_R[0]: 6, W_M_R[1]: 7}
PROJ_OUT_WIDTH = len(STORED_BLOCK) * COL_BLOCK
WIDE_M_A, WIDE_M_R = 2, 3

LANES = 128
SUBLANES = 8
HEADS_PER_LANE_GROUP = LANES // A_HEAD_DIM

PROJ_ROWS = 512
SEQ_BLOCK = 256
CHUNKS_PER_BLOCK = SEQ_BLOCK // CHUNK
ATT_SUB = 2
KEY_BLOCKS = N_LEFT_CHUNKS // CHUNKS_PER_BLOCK + 1
BIAS_RING = (KEY_BLOCKS + 1) * SEQ_BLOCK
VMEM_LIMIT_BYTES = 60 * 1024 * 1024

BF16 = jnp.bfloat16
F32 = jnp.float32


def _resident(shape):
    zeros = (0,) * len(shape)
    return pl.BlockSpec(shape, lambda *_: zeros, pipeline_mode=pl.Buffered(1))


def _sigmoid(x):
    return 0.5 * jnp.tanh(0.5 * x) + 0.5


def _silu(g):
    half = 0.5 * g
    return half * jnp.tanh(half) + half


def _retention_block(r, rows, q_scr, k_scr, qc_scr, kd_scr, v_scr, g_scr, dmat_ref, cdec_ref,
                     gain_ref, o_ref, states):
    pair_lanes = HEADS_PER_LANE_GROUP * R_VAL_DIM
    low_half = lax.broadcasted_iota(jnp.int32, (1, LANES), 1) < R_KEY_DIM
    even_lanes = lax.broadcasted_iota(jnp.int32, (1, pair_lanes), 1) < R_VAL_DIM
    own_block = ((lax.broadcasted_iota(jnp.int32, (LANES, pair_lanes), 0) < R_KEY_DIM)
                 == (lax.broadcasted_iota(jnp.int32, (LANES, pair_lanes), 1) < R_VAL_DIM))
    new_states = []
    for p in range(R_HEADS // HEADS_PER_LANE_GROUP):
        cols = slice(p * LANES, (p + 1) * LANES)
        vcols = slice(p * pair_lanes, (p + 1) * pair_lanes)
        q_pair = q_scr[r, :, cols]
        k_pair = k_scr[r, :, cols]
        v_pair = v_scr[r, :, vcols]
        k_zero = jnp.zeros_like(k_pair)
        k_split = jnp.concatenate([jnp.where(low_half, k_pair, k_zero),
                                   jnp.where(low_half, k_zero, k_pair)], axis=0)
        s = lax.dot_general(q_pair, k_split, (((1,), (1,)), ((), ())),
                            preferred_element_type=F32) * dmat_ref[p]
        v_zero = jnp.zeros_like(v_pair)
        v_diag = jnp.concatenate([jnp.where(even_lanes, v_pair, v_zero),
                                  jnp.where(even_lanes, v_zero, v_pair)], axis=0)
        lhs = jnp.concatenate([s.astype(BF16), qc_scr[r, :, cols]], axis=1)
        rhs = jnp.concatenate([v_diag, states[p].astype(BF16)], axis=0)
        o_pair = jnp.dot(lhs, rhs, preferred_element_type=F32)
        kv = lax.dot_general(kd_scr[r, :, cols], v_pair, (((0,), (0,)), ((), ())),
                             preferred_element_type=F32)
        new_states.append(states[p] * cdec_ref[p] + jnp.where(own_block, kv, 0.0))
        for e in range(HEADS_PER_LANE_GROUP):
            h = p * HEADS_PER_LANE_GROUP + e
            hcols = slice(h * R_VAL_DIM, (h + 1) * R_VAL_DIM)
            o = o_pair[:, e * R_VAL_DIM:(e + 1) * R_VAL_DIM]
            mean = jnp.mean(o, axis=-1, keepdims=True)
            cen = o - mean
            var = jnp.mean(cen * cen, axis=-1, keepdims=True)
            normed = cen * lax.rsqrt(var + GN_EPS) * gain_ref[:, hcols]
            g = g_scr[r, :, hcols].astype(F32)
            o_ref[rows, hcols] = (normed * _silu(g)).astype(BF16)
    return new_states


def _proj_kernel(steps_per_seq, x_ref, gain_ref, w_ref, cos_ref, sin_ref, cross_ref, sdec_ref,
                 dmat_ref, cdec_ref, gn_gain_ref, o_ref, ret_ref, state_ref,
                 q_scr, k_scr, qc_scr, kd_scr, v_scr, g_scr):
    @pl.when(pl.program_id(0) % steps_per_seq == 0)
    def _():
        state_ref[...] = jnp.zeros_like(state_ref)

    lane = lax.broadcasted_iota(jnp.int32, (1, COL_BLOCK), 1)
    first_half = (lane % R_KEY_DIM) < (R_KEY_DIM // 2)
    groups = COL_BLOCK // LANES
    n_pairs = R_HEADS // HEADS_PER_LANE_GROUP
    states = [state_ref[p] for p in range(n_pairs)]
    for r in range(PROJ_ROWS // SEQ_BLOCK):
        rows = slice(r * SEQ_BLOCK, (r + 1) * SEQ_BLOCK)
        x = x_ref[rows, :]
        inv = lax.rsqrt(jnp.mean(x * x, axis=-1, keepdims=True) + NORM_EPS)
        h = (x * inv * gain_ref[...]).astype(BF16)

        for c in RETENTION_W_BLOCKS:
            acc = jnp.dot(h, w_ref[:, _block_cols(c)], preferred_element_type=F32)
            if c in (W_Q_R, W_K_R):
                partner = jnp.where(first_half,
                                    pltpu.roll(acc, COL_BLOCK - R_KEY_DIM // 2, 1),
                                    pltpu.roll(acc, R_KEY_DIM // 2, 1))
                cos = jnp.concatenate([cos_ref[rows, :]] * groups, axis=1)
                sin = jnp.concatenate([sin_ref[rows, :]] * groups, axis=1)
                acc = acc * cos + partner * sin
            if c == W_Q_R:
                acc = acc * (R_KEY_DIM ** -0.5)
                q_scr[r] = acc.astype(BF16)
                qc_scr[r] = (acc * cross_ref[...]).astype(BF16)
            elif c == W_K_R:
                k_scr[r] = acc.astype(BF16)
                kd_scr[r] = (acc * sdec_ref[...]).astype(BF16)
            elif c in W_V_R:
                v_scr[r, :, _block_cols(W_V_R.index(c))] = acc.astype(BF16)
            else:
                g_scr[r, :, _block_cols(W_G_R.index(c))] = acc.astype(BF16)
        states = _retention_block(r, rows, q_scr, k_scr, qc_scr, kd_scr, v_scr, g_scr, dmat_ref,
                                  cdec_ref, gn_gain_ref, ret_ref, states)

        for c, stored in STORED_BLOCK.items():
            acc = jnp.dot(h, w_ref[:, _block_cols(c)], preferred_element_type=F32)
            if c == W_Q_A:
                acc = acc * (A_HEAD_DIM ** -0.5 * LOG2_E)
            o_ref[rows, _block_cols(stored)] = acc.astype(BF16)
    for p in range(n_pairs):
        state_ref[p] = states[p]


def _block_cols(c):
    return slice(c * COL_BLOCK, (c + 1) * COL_BLOCK)


def _projection_and_retention(x2d, gain, w_bf16, cos_t, sin_t, tables, gn_gain, seq_len):
    dmat, cross, sdec, cdec = tables
    n_tok = x2d.shape[0]
    steps_per_seq = seq_len // PROJ_ROWS
    row_groups = PROJ_ROWS // SEQ_BLOCK
    n_pairs = R_HEADS // HEADS_PER_LANE_GROUP
    narrow = pltpu.VMEM((row_groups, SEQ_BLOCK, R_QK_WIDTH), BF16)
    wide = pltpu.VMEM((row_groups, SEQ_BLOCK, R_V_WIDTH), BF16)
    return pl.pallas_call(
        functools.partial(_proj_kernel, steps_per_seq),
        grid=(n_tok // PROJ_ROWS,),
        in_specs=[
            pl.BlockSpec((PROJ_ROWS, D_MODEL), lambda i: (i, 0)),
            _resident((1, D_MODEL)),
            _resident((D_MODEL, IN_WIDTH)),
            pl.BlockSpec((PROJ_ROWS, LANES), lambda i: (i % steps_per_seq, 0)),
            pl.BlockSpec((PROJ_ROWS, LANES), lambda i: (i % steps_per_seq, 0)),
            _resident(cross.shape), _resident(sdec.shape), _resident(dmat.shape),
            _resident(cdec.shape), _resident(gn_gain.shape),
        ],
        out_specs=[pl.BlockSpec((PROJ_ROWS, PROJ_OUT_WIDTH), lambda i: (i, 0)),
                   pl.BlockSpec((PROJ_ROWS, R_V_WIDTH), lambda i: (i, 0))],
        out_shape=[jax.ShapeDtypeStruct((n_tok, PROJ_OUT_WIDTH), BF16),
                   jax.ShapeDtypeStruct((n_tok, R_V_WIDTH), BF16)],
        scratch_shapes=[pltpu.VMEM((n_pairs, LANES, HEADS_PER_LANE_GROUP * R_VAL_DIM), F32),
                        narrow, narrow, narrow, narrow, wide, wide],
        compiler_params=pltpu.CompilerParams(
            dimension_semantics=("arbitrary",), vmem_limit_bytes=VMEM_LIMIT_BYTES),
        name="projection_retention",
    )(x2d, gain, w_bf16, cos_t, sin_t, cross, sdec, dmat, cdec, gn_gain)


def _build_bias_tiles(ring_ref, bias_ref):
    n_keys = KEY_BLOCKS * SEQ_BLOCK
    sublane = lax.broadcasted_iota(jnp.int32, (SUBLANES, BIAS_RING), 0)
    lane = lax.broadcasted_iota(jnp.int32, (SUBLANES, SEQ_BLOCK), 1)
    masked = jnp.full((SUBLANES, SEQ_BLOCK), NEG_INF, F32)
    for h in range(A_HEADS):
        base = ring_ref[h]
        shift = 1
        while shift < SUBLANES:
            base = jnp.where((sublane & shift) != 0, pltpu.roll(base, shift, 1), base)
            shift *= 2
        for a in range(n_keys // SUBLANES):
            key_chunk = a * SUBLANES // CHUNK
            rows = slice(a * SUBLANES, (a + 1) * SUBLANES)
            tile = pltpu.roll(base, a * SUBLANES, 1)[:, :SEQ_BLOCK]
            in_band = ((lane >= (key_chunk - N_LEFT_CHUNKS) * CHUNK)
                       & (lane < (key_chunk + 1) * CHUNK))
            tile = jnp.where(in_band, tile, NEG_INF)
            for v in range(KEY_BLOCKS):
                in_sequence = a * SUBLANES >= (KEY_BLOCKS - 1 - v) * SEQ_BLOCK
                bias_ref[v, h, rows, :] = tile if in_sequence else masked


def _attention_block(sb, q_ref, k_prev_ref, k_cur_ref, v_prev_ref, v_cur_ref, g_ref, bias_ref,
                     s_refs):
    lane = lax.broadcasted_iota(jnp.int32, (1, LANES), 1)
    low_half = lane < A_HEAD_DIM
    halves = (low_half, ~low_half)
    row_half = (lax.broadcasted_iota(jnp.int32, (LANES, 1), 0) < A_HEAD_DIM)
    row_halves = (row_half, ~row_half)
    n_pairs = A_HEADS // HEADS_PER_LANE_GROUP
    rows = slice(sb * SEQ_BLOCK, (sb + 1) * SEQ_BLOCK)
    runtime_zero = jnp.minimum(pl.program_id(1), 0)
    variant = jnp.minimum(pl.program_id(1) * ATT_SUB + sb, KEY_BLOCKS - 1)

    def window(prev_ref, cur_ref, cols):
        blocks = []
        for back in range(KEY_BLOCKS - 1, -1, -1):
            blk = ATT_SUB + sb - back
            ref = prev_ref if blk < ATT_SUB else cur_ref
            start = (blk % ATT_SUB) * SEQ_BLOCK
            blocks.append(ref[start:start + SEQ_BLOCK, cols])
        return jnp.concatenate(blocks, axis=0)

    col_max = []
    for p in range(n_pairs):
        cols = slice(p * LANES, (p + 1) * LANES)
        q_pair = q_ref[rows, cols]
        k_pair = window(k_prev_ref, k_cur_ref, cols)
        for e in range(HEADS_PER_LANE_GROUP):
            h = p * HEADS_PER_LANE_GROUP + e
            q_h = jnp.where(halves[e], q_pair, jnp.zeros_like(q_pair))
            s = lax.dot_general(k_pair, q_h, (((1,), (1,)), ((), ())), preferred_element_type=F32)
            s = s + bias_ref[variant, h]
            s_refs[h][0] = s
            col_max.append(jnp.max(s, axis=0, keepdims=True))

    def second_pass():
        gated = []
        for p in range(n_pairs):
            cols = slice(p * LANES, (p + 1) * LANES)
            v_t = window(v_prev_ref, v_cur_ref, cols).T
            outs = []
            for e in range(HEADS_PER_LANE_GROUP):
                h = p * HEADS_PER_LANE_GROUP + e
                probs = jnp.exp2(s_refs[h][runtime_zero] - col_max[h]).astype(BF16)
                v_h = jnp.where(row_halves[e], v_t, jnp.ones_like(v_t))
                acc = jnp.dot(v_h, probs, preferred_element_type=F32)
                own = slice(e * A_HEAD_DIM, (e + 1) * A_HEAD_DIM)
                other = slice((1 - e) * A_HEAD_DIM, (2 - e) * A_HEAD_DIM)
                outs.append(acc[own] / acc[other])
            o_pair = jnp.concatenate(outs, axis=0).T
            g = g_ref[rows, cols].astype(F32)
            gated.append((o_pair * _silu(g)).astype(BF16))
        return jnp.concatenate(gated, axis=1)

    return second_pass


def _attn_out_kernel(q_ref, k_prev_ref, k_cur_ref, v_prev_ref, v_cur_ref, g_ref, ring_ref,
                     ret_ref, ma_ref, mr_ref, x_ref, wa_ref, wr_ref, wo_ref, fg_ref, o_ref,
                     bias_ref, *s_refs):
    @pl.when((pl.program_id(0) == 0) & (pl.program_id(1) == 0))
    def _():
        _build_bias_tiles(ring_ref, bias_ref)

    second_passes = [
        _attention_block(sb, q_ref, k_prev_ref, k_cur_ref, v_prev_ref, v_cur_ref, g_ref, bias_ref,
                         s_refs[sb * A_HEADS:(sb + 1) * A_HEADS])
        for sb in range(ATT_SUB)]
    for sb in range(ATT_SUB):
        rows = slice(sb * SEQ_BLOCK, (sb + 1) * SEQ_BLOCK)
        att = second_passes[sb]()
        branch_a = jnp.dot(att, wa_ref[...], preferred_element_type=F32)
        branch_r = jnp.dot(ret_ref[rows, :], wr_ref[...], preferred_element_type=F32)
        mixed = (_sigmoid(ma_ref[rows, :].astype(F32)) * branch_a
                 + _sigmoid(mr_ref[rows, :].astype(F32)) * branch_r)
        y = x_ref[rows, :] + jnp.dot(mixed.astype(BF16), wo_ref[...], preferred_element_type=F32)
        inv = lax.rsqrt(jnp.mean(y * y, axis=-1, keepdims=True) + NORM_EPS)
        o_ref[rows, :] = y * inv * fg_ref[...]


def _attention_and_output(proj, ret, x2d, bias_ring, wa, wr, wo, final_gain, batch, seq_len):
    assert ATT_SUB >= KEY_BLOCKS - 1, "the key window must fit in the previous and current step"
    n_keys = KEY_BLOCKS * SEQ_BLOCK
    n_tok = proj.shape[0]
    step_rows = ATT_SUB * SEQ_BLOCK
    steps_per_seq = seq_len // step_rows

    def rows(b, j):
        return b * steps_per_seq + j

    def cur(col, width=COL_BLOCK):
        return pl.BlockSpec((step_rows, width), lambda b, j: (rows(b, j), col))

    def prev(col):
        return pl.BlockSpec((step_rows, COL_BLOCK),
                            lambda b, j: (rows(b, jnp.maximum(j - 1, 0)), col))

    return pl.pallas_call(
        _attn_out_kernel,
        grid=(batch, steps_per_seq),
        in_specs=[cur(COL_Q_A), prev(COL_K_A), cur(COL_K_A), prev(COL_V_A), cur(COL_V_A),
                  cur(COL_G_A), _resident(bias_ring.shape),
                  cur(0, R_V_WIDTH), cur(WIDE_M_A, D_MODEL), cur(WIDE_M_R, D_MODEL),
                  cur(0, D_MODEL),
                  _resident(wa.shape), _resident(wr.shape), _resident(wo.shape),
                  _resident(final_gain.shape)],
        out_specs=cur(0, D_MODEL),
        out_shape=jax.ShapeDtypeStruct((n_tok, D_MODEL), F32),
        scratch_shapes=[pltpu.VMEM((KEY_BLOCKS, A_HEADS, n_keys, SEQ_BLOCK), F32)]
        + [pltpu.VMEM((1, n_keys, SEQ_BLOCK), F32)] * (ATT_SUB * A_HEADS),
        compiler_params=pltpu.CompilerParams(
            dimension_semantics=("arbitrary", "arbitrary"), vmem_limit_bytes=VMEM_LIMIT_BYTES),
        name="attention_output",
    )(proj, proj, proj, proj, proj, proj, bias_ring, ret, proj, proj, x2d, wa, wr, wo, final_gain)


def _rotary_tables(seq_len):
    half = R_KEY_DIM // 2
    inv_freq = jnp.power(ROPE_BASE, -jnp.arange(half, dtype=F32) / half)
    ang = jnp.arange(seq_len).astype(F32)[:, None] * inv_freq[None, :]
    cos, sin = jnp.cos(ang), jnp.sin(ang)
    cos_t = jnp.tile(jnp.concatenate([cos, cos], axis=-1), (1, HEADS_PER_LANE_GROUP))
    sin_t = jnp.tile(jnp.concatenate([-sin, sin], axis=-1), (1, HEADS_PER_LANE_GROUP))
    return cos_t, sin_t


def _take_columns(table, idx):
    pieces, start = [], 0
    while start < len(idx):
        end = start + 1
        if end < len(idx) and idx[end] == idx[start]:
            while end < len(idx) and idx[end] == idx[start]:
                end += 1
            piece = jnp.broadcast_to(table[:, idx[start]:idx[start] + 1], (table.shape[0], end - start))
        else:
            while end < len(idx) and idx[end] == idx[end - 1] + 1:
                end += 1
            piece = table[:, idx[start]:idx[end - 1] + 1]
        pieces.append(piece)
        start = end
    return jnp.concatenate(pieces, axis=1)


def _bias_ring(rel_bias):
    n_keys = KEY_BLOCKS * SEQ_BLOCK
    assert n_keys + SEQ_BLOCK - 1 <= BIAS_RING
    u = np.arange(BIAS_RING)
    t = np.where(u < SEQ_BLOCK, u, u - BIAS_RING)
    idx = np.clip((KEY_BLOCKS - 1) * SEQ_BLOCK + t, -REL_CLIP, REL_CLIP) + REL_CLIP
    ring = _take_columns(rel_bias.astype(F32), idx) * LOG2_E
    return jnp.broadcast_to(ring[:, None, :], (rel_bias.shape[0], SUBLANES, BIAS_RING))


def _retention_tables():
    n_pairs = R_HEADS // HEADS_PER_LANE_GROUP
    log_gamma = jnp.log1p(-jnp.exp2(-5.0 - jnp.arange(R_HEADS, dtype=F32)))
    pos = jnp.arange(SEQ_BLOCK, dtype=F32)
    diff = pos[:, None] - pos[None, :]
    chunk_diff = (np.arange(SEQ_BLOCK)[:, None] // CHUNK) - (np.arange(SEQ_BLOCK)[None, :] // CHUNK)
    expo = jnp.where(chunk_diff == 0, jnp.abs(diff), diff)
    dmat = jnp.where((chunk_diff >= 0)[None], jnp.exp(expo[None] * log_gamma[:, None, None]), 0.0)
    dmat = dmat.reshape(n_pairs, HEADS_PER_LANE_GROUP, SEQ_BLOCK, SEQ_BLOCK)
    dmat = jnp.concatenate([dmat[:, e] for e in range(HEADS_PER_LANE_GROUP)], axis=-1)
    per_lane = lambda tab, width: jnp.repeat(tab.T, width, axis=1)
    cross = per_lane(jnp.exp((pos + 1.0)[None, :] * log_gamma[:, None]), R_KEY_DIM)
    sdec = per_lane(jnp.exp((SEQ_BLOCK - 1.0 - pos)[None, :] * log_gamma[:, None]), R_KEY_DIM)
    block_decay = jnp.exp(SEQ_BLOCK * log_gamma)
    cdec = jnp.broadcast_to(jnp.repeat(block_decay, R_KEY_DIM).reshape(n_pairs, LANES, 1),
                            (n_pairs, LANES, HEADS_PER_LANE_GROUP * R_VAL_DIM))
    return dmat.astype(F32), cross, sdec, cdec


def kernel(x, norm_gain, w_in, rel_bias, gn_gain, w_out_attn, w_out_ret, w_out, final_gain):
    batch, seq_len, d_model = x.shape
    assert d_model == D_MODEL and norm_gain.shape[0] == 1, "single-layer block of width D_MODEL"
    assert seq_len % PROJ_ROWS == 0 and PROJ_ROWS % SEQ_BLOCK == 0
    assert seq_len % (ATT_SUB * SEQ_BLOCK) == 0
    x2d = x.reshape(batch * seq_len, D_MODEL)
    cos_t, sin_t = _rotary_tables(seq_len)
    proj, ret = _projection_and_retention(x2d, norm_gain[0][None, :], w_in[0].astype(BF16), cos_t, sin_t,
                                          _retention_tables(), gn_gain[0][None, :], seq_len)
    out = _attention_and_output(proj, ret, x2d, _bias_ring(rel_bias[0]), w_out_attn[0].astype(BF16),
                                w_out_ret[0].astype(BF16), w_out[0].astype(BF16),
                                final_gain[None, :], batch, seq_len)
    return out.reshape(batch, seq_len, D_MODEL)
```

```python
import functools

import numpy as np
import jax
import jax.numpy as jnp
from jax import lax
from jax.experimental import pallas as pl
from jax.experimental.pallas import tpu as pltpu

D_MODEL = 1024
CHUNK = 64
N_LEFT_CHUNKS = 8
A_HEADS = 8
A_HEAD_DIM = 64
A_WIDTH = A_HEADS * A_HEAD_DIM
REL_CLIP = 128
R_HEADS = 8
R_KEY_DIM = 64
R_VAL_DIM = 128
R_QK_WIDTH = R_HEADS * R_KEY_DIM
R_V_WIDTH = R_HEADS * R_VAL_DIM
ROPE_BASE = 10000.0
NORM_EPS = 1e-6
GN_EPS = 1e-5
NEG_INF = -1e30
LOG2_E = 1.4426950408889634
IN_WIDTH = 4 * A_WIDTH + 2 * R_QK_WIDTH + 2 * R_V_WIDTH + 2 * D_MODEL

COL_BLOCK = 512
N_COL_BLOCKS = IN_WIDTH // COL_BLOCK
W_Q_A, W_K_A, W_V_A, W_G_A, W_Q_R, W_K_R = 0, 1, 2, 3, 4, 5
W_V_R, W_G_R, W_M_A, W_M_R = (6, 7), (8, 9), (10, 11), (12, 13)
RETENTION_W_BLOCKS = (W_Q_R, W_K_R) + W_V_R + W_G_R
COL_Q_A, COL_K_A, COL_V_A, COL_G_A = 0, 1, 2, 3
STORED_BLOCK = {W_Q_A: COL_Q_A, W_K_A: COL_K_A, W_V_A: COL_V_A, W_G_A: COL_G_A,
                W_M_A[0]: 4, W_M_A[1]: 5, W_M_R[0]: 6, W_M_R[1]: 7}
PROJ_OUT_WIDTH = len(STORED_BLOCK) * COL_BLOCK
WIDE_M_A, WIDE_M_R = 2, 3

LANES = 128
SUBLANES = 8
HEADS_PER_LANE_GROUP = LANES // A_HEAD_DIM

PROJ_ROWS = 512
SEQ_BLOCK = 256
RET_BLOCK = 128
CHUNKS_PER_BLOCK = SEQ_BLOCK // CHUNK
ATT_SUB = 2
KEY_BLOCKS = N_LEFT_CHUNKS // CHUNKS_PER_BLOCK + 1
BIAS_RING = (KEY_BLOCKS + 1) * SEQ_BLOCK
VMEM_LIMIT_BYTES = 60 * 1024 * 1024

BF16 = jnp.bfloat16
F32 = jnp.float32


def _resident(shape):
    zeros = (0,) * len(shape)
    return pl.BlockSpec(shape, lambda *_: zeros, pipeline_mode=pl.Buffered(1))


def _sigmoid(x):
    return 0.5 * jnp.tanh(0.5 * x) + 0.5


def _silu(g):
    half = 0.5 * g
    return half * jnp.tanh(half) + half


def _retention_block(r, sub, out_rows, q_scr, k_scr, qc_scr, kd_scr, v_scr, g_scr, dmat_ref, cdec_ref,
                     gain_ref, o_ref, states):
    pair_lanes = HEADS_PER_LANE_GROUP * R_VAL_DIM
    low_half = lax.broadcasted_iota(jnp.int32, (1, LANES), 1) < R_KEY_DIM
    even_lanes = lax.broadcasted_iota(jnp.int32, (1, pair_lanes), 1) < R_VAL_DIM
    own_block = ((lax.broadcasted_iota(jnp.int32, (LANES, pair_lanes), 0) < R_KEY_DIM)
                 == (lax.broadcasted_iota(jnp.int32, (LANES, pair_lanes), 1) < R_VAL_DIM))
    new_states = []
    for p in range(R_HEADS // HEADS_PER_LANE_GROUP):
        cols = slice(p * LANES, (p + 1) * LANES)
        vcols = slice(p * pair_lanes, (p + 1) * pair_lanes)
        q_pair = q_scr[r, sub, cols]
        k_pair = k_scr[r, sub, cols]
        v_pair = v_scr[r, sub, vcols]
        k_zero = jnp.zeros_like(k_pair)
        k_split = jnp.concatenate([jnp.where(low_half, k_pair, k_zero),
                                   jnp.where(low_half, k_zero, k_pair)], axis=0)
        s = lax.dot_general(q_pair, k_split, (((1,), (1,)), ((), ())),
                            preferred_element_type=F32) * dmat_ref[p]
        v_zero = jnp.zeros_like(v_pair)
        v_diag = jnp.concatenate([jnp.where(even_lanes, v_pair, v_zero),
                                  jnp.where(even_lanes, v_zero, v_pair)], axis=0)
        lhs = jnp.concatenate([s.astype(BF16), qc_scr[r, sub, cols]], axis=1)
        rhs = jnp.concatenate([v_diag, states[p].astype(BF16)], axis=0)
        o_pair = jnp.dot(lhs, rhs, preferred_element_type=F32)
        kv = lax.dot_general(kd_scr[r, sub, cols], v_pair, (((0,), (0,)), ((), ())),
                             preferred_element_type=F32)
        new_states.append(states[p] * cdec_ref[p] + jnp.where(own_block, kv, 0.0))
        for e in range(HEADS_PER_LANE_GROUP):
            h = p * HEADS_PER_LANE_GROUP + e
            hcols = slice(h * R_VAL_DIM, (h + 1) * R_VAL_DIM)
            o = o_pair[:, e * R_VAL_DIM:(e + 1) * R_VAL_DIM]
            mean = jnp.mean(o, axis=-1, keepdims=True)
            cen = o - mean
            var = jnp.mean(cen * cen, axis=-1, keepdims=True)
            normed = cen * lax.rsqrt(var + GN_EPS) * gain_ref[:, hcols]
            g = g_scr[r, sub, hcols].astype(F32)
            o_ref[out_rows, hcols] = (normed * _silu(g)).astype(BF16)
    return new_states


def _proj_kernel(steps_per_seq, x_ref, gain_ref, w_ref, cos_ref, sin_ref, cross_ref, sdec_ref,
                 dmat_ref, cdec_ref, gn_gain_ref, o_ref, ret_ref, state_ref,
                 q_scr, k_scr, qc_scr, kd_scr, v_scr, g_scr):
    @pl.when(pl.program_id(0) % steps_per_seq == 0)
    def _():
        state_ref[...] = jnp.zeros_like(state_ref)

    lane = lax.broadcasted_iota(jnp.int32, (1, COL_BLOCK), 1)
    first_half = (lane % R_KEY_DIM) < (R_KEY_DIM // 2)
    groups = COL_BLOCK // LANES
    n_pairs = R_HEADS // HEADS_PER_LANE_GROUP
    states = [state_ref[p] for p in range(n_pairs)]
    for r in range(PROJ_ROWS // SEQ_BLOCK):
        rows = slice(r * SEQ_BLOCK, (r + 1) * SEQ_BLOCK)
        x = x_ref[rows, :]
        inv = lax.rsqrt(jnp.mean(x * x, axis=-1, keepdims=True) + NORM_EPS)
        h = (x * inv * gain_ref[...]).astype(BF16)

        for c in RETENTION_W_BLOCKS:
            acc = jnp.dot(h, w_ref[:, _block_cols(c)], preferred_element_type=F32)
            if c in (W_Q_R, W_K_R):
                partner = jnp.where(first_half,
                                    pltpu.roll(acc, COL_BLOCK - R_KEY_DIM // 2, 1),
                                    pltpu.roll(acc, R_KEY_DIM // 2, 1))
                cos = jnp.concatenate([cos_ref[rows, :]] * groups, axis=1)
                sin = jnp.concatenate([sin_ref[rows, :]] * groups, axis=1)
                acc = acc * cos + partner * sin
            if c == W_Q_R:
                acc = acc * (R_KEY_DIM ** -0.5)
                q_scr[r] = acc.astype(BF16)
                qc_scr[r] = (acc * cross_ref[...]).astype(BF16)
            elif c == W_K_R:
                k_scr[r] = acc.astype(BF16)
                kd_scr[r] = (acc * sdec_ref[...]).astype(BF16)
            elif c in W_V_R:
                v_scr[r, :, _block_cols(W_V_R.index(c))] = acc.astype(BF16)
            else:
                g_scr[r, :, _block_cols(W_G_R.index(c))] = acc.astype(BF16)
        for b in range(SEQ_BLOCK // RET_BLOCK):
            sub = slice(b * RET_BLOCK, (b + 1) * RET_BLOCK)
            out_rows = slice(r * SEQ_BLOCK + b * RET_BLOCK, r * SEQ_BLOCK + (b + 1) * RET_BLOCK)
            states = _retention_block(r, sub, out_rows, q_scr, k_scr, qc_scr, kd_scr, v_scr, g_scr,
                                      dmat_ref, cdec_ref, gn_gain_ref, ret_ref, states)

        for c, stored in STORED_BLOCK.items():
            acc = jnp.dot(h, w_ref[:, _block_cols(c)], preferred_element_type=F32)
            if c == W_Q_A:
                acc = acc * (A_HEAD_DIM ** -0.5 * LOG2_E)
            o_ref[rows, _block_cols(stored)] = acc.astype(BF16)
    for p in range(n_pairs):
        state_ref[p] = states[p]


def _block_cols(c):
    return slice(c * COL_BLOCK, (c + 1) * COL_BLOCK)


def _projection_and_retention(x2d, gain, w_bf16, cos_t, sin_t, tables, gn_gain, seq_len):
    dmat, cross, sdec, cdec = tables
    n_tok = x2d.shape[0]
    steps_per_seq = seq_len // PROJ_ROWS
    row_groups = PROJ_ROWS // SEQ_BLOCK
    n_pairs = R_HEADS // HEADS_PER_LANE_GROUP
    narrow = pltpu.VMEM((row_groups, SEQ_BLOCK, R_QK_WIDTH), BF16)
    wide = pltpu.VMEM((row_groups, SEQ_BLOCK, R_V_WIDTH), BF16)
    return pl.pallas_call(
        functools.partial(_proj_kernel, steps_per_seq),
        grid=(n_tok // PROJ_ROWS,),
        in_specs=[
            pl.BlockSpec((PROJ_ROWS, D_MODEL), lambda i: (i, 0)),
            _resident((1, D_MODEL)),
            _resident((D_MODEL, IN_WIDTH)),
            pl.BlockSpec((PROJ_ROWS, LANES), lambda i: (i % steps_per_seq, 0)),
            pl.BlockSpec((PROJ_ROWS, LANES), lambda i: (i % steps_per_seq, 0)),
            _resident(cross.shape), _resident(sdec.shape), _resident(dmat.shape),
            _resident(cdec.shape), _resident(gn_gain.shape),
        ],
        out_specs=[pl.BlockSpec((PROJ_ROWS, PROJ_OUT_WIDTH), lambda i: (i, 0)),
                   pl.BlockSpec((PROJ_ROWS, R_V_WIDTH), lambda i: (i, 0))],
        out_shape=[jax.ShapeDtypeStruct((n_tok, PROJ_OUT_WIDTH), BF16),
                   jax.ShapeDtypeStruct((n_tok, R_V_WIDTH), BF16)],
        scratch_shapes=[pltpu.VMEM((n_pairs, LANES, HEADS_PER_LANE_GROUP * R_VAL_DIM), F32),
                        narrow, narrow, narrow, narrow, wide, wide],
        compiler_params=pltpu.CompilerParams(
            dimension_semantics=("arbitrary",), vmem_limit_bytes=VMEM_LIMIT_BYTES),
        name="projection_retention",
    )(x2d, gain, w_bf16, cos_t, sin_t, cross, sdec, dmat, cdec, gn_gain)


def _build_bias_tiles(ring_ref, bias_ref):
    n_keys = KEY_BLOCKS * SEQ_BLOCK
    sublane = lax.broadcasted_iota(jnp.int32, (SUBLANES, BIAS_RING), 0)
    lane = lax.broadcasted_iota(jnp.int32, (SUBLANES, SEQ_BLOCK), 1)
    masked = jnp.full((SUBLANES, SEQ_BLOCK), NEG_INF, F32)
    for h in range(A_HEADS):
        base = ring_ref[h]
        shift = 1
        while shift < SUBLANES:
            base = jnp.where((sublane & shift) != 0, pltpu.roll(base, shift, 1), base)
            shift *= 2
        for a in range(n_keys // SUBLANES):
            key_chunk = a * SUBLANES // CHUNK
            rows = slice(a * SUBLANES, (a + 1) * SUBLANES)
            tile = pltpu.roll(base, a * SUBLANES, 1)[:, :SEQ_BLOCK]
            in_band = ((lane >= (key_chunk - N_LEFT_CHUNKS) * CHUNK)
                       & (lane < (key_chunk + 1) * CHUNK))
            tile = jnp.where(in_band, tile, NEG_INF)
            for v in range(KEY_BLOCKS):
                in_sequence = a * SUBLANES >= (KEY_BLOCKS - 1 - v) * SEQ_BLOCK
                bias_ref[v, h, rows, :] = tile if in_sequence else masked


def _attention_block(sb, q_ref, k_prev_ref, k_cur_ref, v_prev_ref, v_cur_ref, g_ref, bias_ref,
                     s_refs):
    lane = lax.broadcasted_iota(jnp.int32, (1, LANES), 1)
    low_half = lane < A_HEAD_DIM
    halves = (low_half, ~low_half)
    row_half = (lax.broadcasted_iota(jnp.int32, (LANES, 1), 0) < A_HEAD_DIM)
    row_halves = (row_half, ~row_half)
    n_pairs = A_HEADS // HEADS_PER_LANE_GROUP
    rows = slice(sb * SEQ_BLOCK, (sb + 1) * SEQ_BLOCK)
    runtime_zero = jnp.minimum(pl.program_id(1), 0)
    variant = jnp.minimum(pl.program_id(1) * ATT_SUB + sb, KEY_BLOCKS - 1)

    def window(prev_ref, cur_ref, cols):
        blocks = []
        for back in range(KEY_BLOCKS - 1, -1, -1):
            blk = ATT_SUB + sb - back
            ref = prev_ref if blk < ATT_SUB else cur_ref
            start = (blk % ATT_SUB) * SEQ_BLOCK
            blocks.append(ref[start:start + SEQ_BLOCK, cols])
        return jnp.concatenate(blocks, axis=0)

    col_max = []
    for p in range(n_pairs):
        cols = slice(p * LANES, (p + 1) * LANES)
        q_pair = q_ref[rows, cols]
        k_pair = window(k_prev_ref, k_cur_ref, cols)
        for e in range(HEADS_PER_LANE_GROUP):
            h = p * HEADS_PER_LANE_GROUP + e
            q_h = jnp.where(halves[e], q_pair, jnp.zeros_like(q_pair))
            s = lax.dot_general(k_pair, q_h, (((1,), (1,)), ((), ())), preferred_element_type=F32)
            s = s + bias_ref[variant, h]
            s_refs[h][0] = s
            col_max.append(jnp.max(s, axis=0, keepdims=True))

    def second_pass():
        gated = []
        for p in range(n_pairs):
            cols = slice(p * LANES, (p + 1) * LANES)
            v_t = window(v_prev_ref, v_cur_ref, cols).T
            outs = []
            for e in range(HEADS_PER_LANE_GROUP):
                h = p * HEADS_PER_LANE_GROUP + e
                probs = jnp.exp2(s_refs[h][runtime_zero] - col_max[h]).astype(BF16)
                v_h = jnp.where(row_halves[e], v_t, jnp.ones_like(v_t))
                acc = jnp.dot(v_h, probs, preferred_element_type=F32)
                own = slice(e * A_HEAD_DIM, (e + 1) * A_HEAD_DIM)
                other = slice((1 - e) * A_HEAD_DIM, (2 - e) * A_HEAD_DIM)
                outs.append(acc[own] / acc[other])
            o_pair = jnp.concatenate(outs, axis=0).T
            g = g_ref[rows, cols].astype(F32)
            gated.append((o_pair * _silu(g)).astype(BF16))
        return jnp.concatenate(gated, axis=1)

    return second_pass


def _attn_out_kernel(q_ref, k_prev_ref, k_cur_ref, v_prev_ref, v_cur_ref, g_ref, ring_ref,
                     ret_ref, ma_ref, mr_ref, x_ref, wa_ref, wr_ref, wo_ref, fg_ref, o_ref,
                     bias_ref, *s_refs):
    @pl.when((pl.program_id(0) == 0) & (pl.program_id(1) == 0))
    def _():
        _build_bias_tiles(ring_ref, bias_ref)

    second_passes = [
        _attention_block(sb, q_ref, k_prev_ref, k_cur_ref, v_prev_ref, v_cur_ref, g_ref, bias_ref,
                         s_refs[sb * A_HEADS:(sb + 1) * A_HEADS])
        for sb in range(ATT_SUB)]
    for sb in range(ATT_SUB):
        rows = slice(sb * SEQ_BLOCK, (sb + 1) * SEQ_BLOCK)
        att = second_passes[sb]()
        branch_a = jnp.dot(att, wa_ref[...], preferred_element_type=F32)
        branch_r = jnp.dot(ret_ref[rows, :], wr_ref[...], preferred_element_type=F32)
        mixed = (_sigmoid(ma_ref[rows, :].astype(F32)) * branch_a
                 + _sigmoid(mr_ref[rows, :].astype(F32)) * branch_r)
        y = x_ref[rows, :] + jnp.dot(mixed.astype(BF16), wo_ref[...], preferred_element_type=F32)
        inv = lax.rsqrt(jnp.mean(y * y, axis=-1, keepdims=True) + NORM_EPS)
        o_ref[rows, :] = y * inv * fg_ref[...]


def _attention_and_output(proj, ret, x2d, bias_ring, wa, wr, wo, final_gain, batch, seq_len):
    assert ATT_SUB >= KEY_BLOCKS - 1, "the key window must fit in the previous and current step"
    n_keys = KEY_BLOCKS * SEQ_BLOCK
    n_tok = proj.shape[0]
    step_rows = ATT_SUB * SEQ_BLOCK
    steps_per_seq = seq_len // step_rows

    def rows(b, j):
        return b * steps_per_seq + j

    def cur(col, width=COL_BLOCK):
        return pl.BlockSpec((step_rows, width), lambda b, j: (rows(b, j), col))

    def prev(col):
        return pl.BlockSpec((step_rows, COL_BLOCK),
                            lambda b, j: (rows(b, jnp.maximum(j - 1, 0)), col))

    return pl.pallas_call(
        _attn_out_kernel,
        grid=(batch, steps_per_seq),
        in_specs=[cur(COL_Q_A), prev(COL_K_A), cur(COL_K_A), prev(COL_V_A), cur(COL_V_A),
                  cur(COL_G_A), _resident(bias_ring.shape),
                  cur(0, R_V_WIDTH), cur(WIDE_M_A, D_MODEL), cur(WIDE_M_R, D_MODEL),
                  cur(0, D_MODEL),
                  _resident(wa.shape), _resident(wr.shape), _resident(wo.shape),
                  _resident(final_gain.shape)],
        out_specs=cur(0, D_MODEL),
        out_shape=jax.ShapeDtypeStruct((n_tok, D_MODEL), F32),
        scratch_shapes=[pltpu.VMEM((KEY_BLOCKS, A_HEADS, n_keys, SEQ_BLOCK), F32)]
        + [pltpu.VMEM((1, n_keys, SEQ_BLOCK), F32)] * (ATT_SUB * A_HEADS),
        compiler_params=pltpu.CompilerParams(
            dimension_semantics=("arbitrary", "arbitrary"), vmem_limit_bytes=VMEM_LIMIT_BYTES),
        name="attention_output",
    )(proj, proj, proj, proj, proj, proj, bias_ring, ret, proj, proj, x2d, wa, wr, wo, final_gain)


def _rotary_tables(seq_len):
    half = R_KEY_DIM // 2
    inv_freq = jnp.power(ROPE_BASE, -jnp.arange(half, dtype=F32) / half)
    ang = jnp.arange(seq_len).astype(F32)[:, None] * inv_freq[None, :]
    cos, sin = jnp.cos(ang), jnp.sin(ang)
    cos_t = jnp.tile(jnp.concatenate([cos, cos], axis=-1), (1, HEADS_PER_LANE_GROUP))
    sin_t = jnp.tile(jnp.concatenate([-sin, sin], axis=-1), (1, HEADS_PER_LANE_GROUP))
    return cos_t, sin_t


def _take_columns(table, idx):
    pieces, start = [], 0
    while start < len(idx):
        end = start + 1
        if end < len(idx) and idx[end] == idx[start]:
            while end < len(idx) and idx[end] == idx[start]:
                end += 1
            piece = jnp.broadcast_to(table[:, idx[start]:idx[start] + 1], (table.shape[0], end - start))
        else:
            while end < len(idx) and idx[end] == idx[end - 1] + 1:
                end += 1
            piece = table[:, idx[start]:idx[end - 1] + 1]
        pieces.append(piece)
        start = end
    return jnp.concatenate(pieces, axis=1)


def _bias_ring(rel_bias):
    n_keys = KEY_BLOCKS * SEQ_BLOCK
    assert n_keys + SEQ_BLOCK - 1 <= BIAS_RING
    u = np.arange(BIAS_RING)
    t = np.where(u < SEQ_BLOCK, u, u - BIAS_RING)
    idx = np.clip((KEY_BLOCKS - 1) * SEQ_BLOCK + t, -REL_CLIP, REL_CLIP) + REL_CLIP
    ring = _take_columns(rel_bias.astype(F32), idx) * LOG2_E
    return jnp.broadcast_to(ring[:, None, :], (rel_bias.shape[0], SUBLANES, BIAS_RING))


def _retention_tables():
    n_pairs = R_HEADS // HEADS_PER_LANE_GROUP
    log_gamma = jnp.log1p(-jnp.exp2(-5.0 - jnp.arange(R_HEADS, dtype=F32)))
    pos = jnp.arange(RET_BLOCK, dtype=F32)
    diff = pos[:, None] - pos[None, :]
    chunk_diff = (np.arange(RET_BLOCK)[:, None] // CHUNK) - (np.arange(RET_BLOCK)[None, :] // CHUNK)
    expo = jnp.where(chunk_diff == 0, jnp.abs(diff), diff)
    dmat = jnp.where((chunk_diff >= 0)[None], jnp.exp(expo[None] * log_gamma[:, None, None]), 0.0)
    dmat = dmat.reshape(n_pairs, HEADS_PER_LANE_GROUP, RET_BLOCK, RET_BLOCK)
    dmat = jnp.concatenate([dmat[:, e] for e in range(HEADS_PER_LANE_GROUP)], axis=-1)
    per_lane = lambda tab, width: jnp.tile(jnp.repeat(tab.T, width, axis=1), (SEQ_BLOCK // RET_BLOCK, 1))
    cross = per_lane(jnp.exp((pos + 1.0)[None, :] * log_gamma[:, None]), R_KEY_DIM)
    sdec = per_lane(jnp.exp((RET_BLOCK - 1.0 - pos)[None, :] * log_gamma[:, None]), R_KEY_DIM)
    block_decay = jnp.exp(RET_BLOCK * log_gamma)
    cdec = jnp.broadcast_to(jnp.repeat(block_decay, R_KEY_DIM).reshape(n_pairs, LANES, 1),
                            (n_pairs, LANES, HEADS_PER_LANE_GROUP * R_VAL_DIM))
    return dmat.astype(F32), cross, sdec, cdec


def kernel(x, norm_gain, w_in, rel_bias, gn_gain, w_out_attn, w_out_ret, w_out, final_gain):
    batch, seq_len, d_model = x.shape
    assert d_model == D_MODEL and norm_gain.shape[0] == 1, "single-layer block of width D_MODEL"
    assert seq_len % PROJ_ROWS == 0 and PROJ_ROWS % SEQ_BLOCK == 0
    assert seq_len % (ATT_SUB * SEQ_BLOCK) == 0
    x2d = x.reshape(batch * seq_len, D_MODEL)
    cos_t, sin_t = _rotary_tables(seq_len)
    proj, ret = _projection_and_retention(x2d, norm_gain[0][None, :], w_in[0].astype(BF16), cos_t, sin_t,
                                          _retention_tables(), gn_gain[0][None, :], seq_len)
    out = _attention_and_output(proj, ret, x2d, _bias_ring(rel_bias[0]), w_out_attn[0].astype(BF16),
                                w_out_ret[0].astype(BF16), w_out[0].astype(BF16),
                                final_gain[None, :], batch, seq_len)
    return out.reshape(batch, seq_len, D_MODEL)
```

```python
import functools

import numpy as np
import jax
import jax.numpy as jnp
from jax import lax
from jax.experimental import pallas as pl
from jax.experimental.pallas import tpu as pltpu

D_MODEL = 1024
CHUNK = 64
N_LEFT_CHUNKS = 8
A_HEADS = 8
A_HEAD_DIM = 64
A_WIDTH = A_HEADS * A_HEAD_DIM
REL_CLIP = 128
R_HEADS = 8
R_KEY_DIM = 64
R_VAL_DIM = 128
R_QK_WIDTH = R_HEADS * R_KEY_DIM
R_V_WIDTH = R_HEADS * R_VAL_DIM
ROPE_BASE = 10000.0
NORM_EPS = 1e-6
GN_EPS = 1e-5
NEG_INF = -1e30
LOG2_E = 1.4426950408889634
IN_WIDTH = 4 * A_WIDTH + 2 * R_QK_WIDTH + 2 * R_V_WIDTH + 2 * D_MODEL

COL_BLOCK = 512
N_COL_BLOCKS = IN_WIDTH // COL_BLOCK
W_Q_A, W_K_A, W_V_A, W_G_A, W_Q_R, W_K_R = 0, 1, 2, 3, 4, 5
W_V_R, W_G_R, W_M_A, W_M_R = (6, 7), (8, 9), (10, 11), (12, 13)
RETENTION_W_BLOCKS = (W_Q_R, W_K_R) + W_V_R + W_G_R
COL_Q_A, COL_K_A, COL_V_A, COL_G_A = 0, 1, 2, 3
STORED_BLOCK = {W_Q_A: COL_Q_A, W_K_A: COL_K_A, W_V_A: COL_V_A, W_G_A: COL_G_A,
                W_M_A[0]: 4, W_M_A[1]: 5, W_M_R[0]: 6, W_M_R[1]: 7}
PROJ_OUT_WIDTH = len(STORED_BLOCK) * COL_BLOCK
WIDE_M_A, WIDE_M_R = 2, 3

LANES = 128
SUBLANES = 8
HEADS_PER_LANE_GROUP = LANES // A_HEAD_DIM

PROJ_ROWS = 512
SEQ_BLOCK = 256
RET_BLOCK = 128
CHUNKS_PER_BLOCK = SEQ_BLOCK // CHUNK
ATT_SUB = 2
KEY_BLOCKS = N_LEFT_CHUNKS // CHUNKS_PER_BLOCK + 1
BIAS_RING = (KEY_BLOCKS + 1) * SEQ_BLOCK
VMEM_LIMIT_BYTES = 60 * 1024 * 1024

BF16 = jnp.bfloat16
F32 = jnp.float32


def _resident(shape):
    zeros = (0,) * len(shape)
    return pl.BlockSpec(shape, lambda *_: zeros, pipeline_mode=pl.Buffered(1))


def _sigmoid(x):
    return 0.5 * jnp.tanh(0.5 * x) + 0.5


def _silu(g):
    half = 0.5 * g
    return half * jnp.tanh(half) + half


def _retention_block(r, sub, out_rows, q_scr, k_scr, qc_scr, kd_scr, v_scr, g_scr, dmat_ref, cdec_ref,
                     gain_ref, o_ref, states):
    pair_lanes = HEADS_PER_LANE_GROUP * R_VAL_DIM
    low_half = lax.broadcasted_iota(jnp.int32, (1, LANES), 1) < R_KEY_DIM
    even_lanes = lax.broadcasted_iota(jnp.int32, (1, pair_lanes), 1) < R_VAL_DIM
    own_block = ((lax.broadcasted_iota(jnp.int32, (LANES, pair_lanes), 0) < R_KEY_DIM)
                 == (lax.broadcasted_iota(jnp.int32, (LANES, pair_lanes), 1) < R_VAL_DIM))
    new_states = []
    for p in range(R_HEADS // HEADS_PER_LANE_GROUP):
        cols = slice(p * LANES, (p + 1) * LANES)
        vcols = slice(p * pair_lanes, (p + 1) * pair_lanes)
        q_pair = q_scr[r, sub, cols]
        k_pair = k_scr[r, sub, cols]
        v_pair = v_scr[r, sub, vcols]
        k_zero = jnp.zeros_like(k_pair)
        k_split = jnp.concatenate([jnp.where(low_half, k_pair, k_zero),
                                   jnp.where(low_half, k_zero, k_pair)], axis=0)
        s = lax.dot_general(q_pair, k_split, (((1,), (1,)), ((), ())),
                            preferred_element_type=F32) * dmat_ref[p]
        v_zero = jnp.zeros_like(v_pair)
        v_diag = jnp.concatenate([jnp.where(even_lanes, v_pair, v_zero),
                                  jnp.where(even_lanes, v_zero, v_pair)], axis=0)
        lhs = jnp.concatenate([s.astype(BF16), qc_scr[r, sub, cols]], axis=1)
        rhs = jnp.concatenate([v_diag, states[p].astype(BF16)], axis=0)
        o_pair = jnp.dot(lhs, rhs, preferred_element_type=F32)
        kv = lax.dot_general(kd_scr[r, sub, cols], v_pair, (((0,), (0,)), ((), ())),
                             preferred_element_type=F32)
        new_states.append(states[p] * cdec_ref[p] + jnp.where(own_block, kv, 0.0))
        for e in range(HEADS_PER_LANE_GROUP):
            h = p * HEADS_PER_LANE_GROUP + e
            hcols = slice(h * R_VAL_DIM, (h + 1) * R_VAL_DIM)
            o = o_pair[:, e * R_VAL_DIM:(e + 1) * R_VAL_DIM]
            mean = jnp.mean(o, axis=-1, keepdims=True)
            cen = o - mean
            var = jnp.mean(cen * cen, axis=-1, keepdims=True)
            normed = cen * lax.rsqrt(var + GN_EPS) * gain_ref[:, hcols]
            g = g_scr[r, sub, hcols].astype(F32)
            o_ref[out_rows, hcols] = (normed * _silu(g)).astype(BF16)
    return new_states


def _proj_kernel(steps_per_seq, x_ref, gain_ref, w_ref, cos_ref, sin_ref, cross_ref, sdec_ref,
                 dmat_ref, cdec_ref, gn_gain_ref, o_ref, ret_ref, state_ref,
                 q_scr, k_scr, qc_scr, kd_scr, v_scr, g_scr):
    @pl.when(pl.program_id(0) % steps_per_seq == 0)
    def _():
        state_ref[...] = jnp.zeros_like(state_ref)

    lane = lax.broadcasted_iota(jnp.int32, (1, COL_BLOCK), 1)
    first_half = (lane % R_KEY_DIM) < (R_KEY_DIM // 2)
    groups = COL_BLOCK // LANES
    n_pairs = R_HEADS // HEADS_PER_LANE_GROUP
    states = [state_ref[p] for p in range(n_pairs)]
    for r in range(PROJ_ROWS // SEQ_BLOCK):
        rows = slice(r * SEQ_BLOCK, (r + 1) * SEQ_BLOCK)
        x = x_ref[rows, :]
        inv = lax.rsqrt(jnp.mean(x * x, axis=-1, keepdims=True) + NORM_EPS)
        h = (x * inv * gain_ref[...]).astype(BF16)

        for c in RETENTION_W_BLOCKS:
            acc = jnp.dot(h, w_ref[:, _block_cols(c)], preferred_element_type=F32)
            if c in (W_Q_R, W_K_R):
                partner = jnp.where(first_half,
                                    pltpu.roll(acc, COL_BLOCK - R_KEY_DIM // 2, 1),
                                    pltpu.roll(acc, R_KEY_DIM // 2, 1))
                cos = jnp.concatenate([cos_ref[rows, :]] * groups, axis=1)
                sin = jnp.concatenate([sin_ref[rows, :]] * groups, axis=1)
                acc = acc * cos + partner * sin
            if c == W_Q_R:
                acc = acc * (R_KEY_DIM ** -0.5)
                q_scr[r] = acc.astype(BF16)
                qc_scr[r] = (acc * cross_ref[...]).astype(BF16)
            elif c == W_K_R:
                k_scr[r] = acc.astype(BF16)
                kd_scr[r] = (acc * sdec_ref[...]).astype(BF16)
            elif c in W_V_R:
                v_scr[r, :, _block_cols(W_V_R.index(c))] = acc.astype(BF16)
            else:
                g_scr[r, :, _block_cols(W_G_R.index(c))] = acc.astype(BF16)
        for b in range(SEQ_BLOCK // RET_BLOCK):
            sub = slice(b * RET_BLOCK, (b + 1) * RET_BLOCK)
            out_rows = slice(r * SEQ_BLOCK + b * RET_BLOCK, r * SEQ_BLOCK + (b + 1) * RET_BLOCK)
            states = _retention_block(r, sub, out_rows, q_scr, k_scr, qc_scr, kd_scr, v_scr, g_scr,
                                      dmat_ref, cdec_ref, gn_gain_ref, ret_ref, states)

        for c, stored in STORED_BLOCK.items():
            acc = jnp.dot(h, w_ref[:, _block_cols(c)], preferred_element_type=F32)
            if c == W_Q_A:
                acc = acc * (A_HEAD_DIM ** -0.5 * LOG2_E)
            o_ref[rows, _block_cols(stored)] = acc.astype(BF16)
    for p in range(n_pairs):
        state_ref[p] = states[p]


def _block_cols(c):
    return slice(c * COL_BLOCK, (c + 1) * COL_BLOCK)


def _projection_and_retention(x2d, gain, w_bf16, cos_t, sin_t, tables, gn_gain, seq_len):
    dmat, cross, sdec, cdec = tables
    n_tok = x2d.shape[0]
    steps_per_seq = seq_len // PROJ_ROWS
    row_groups = PROJ_ROWS // SEQ_BLOCK
    n_pairs = R_HEADS // HEADS_PER_LANE_GROUP
    narrow = pltpu.VMEM((row_groups, SEQ_BLOCK, R_QK_WIDTH), BF16)
    wide = pltpu.VMEM((row_groups, SEQ_BLOCK, R_V_WIDTH), BF16)
    return pl.pallas_call(
        functools.partial(_proj_kernel, steps_per_seq),
        grid=(n_tok // PROJ_ROWS,),
        in_specs=[
            pl.BlockSpec((PROJ_ROWS, D_MODEL), lambda i: (i, 0)),
            _resident((1, D_MODEL)),
            _resident((D_MODEL, IN_WIDTH)),
            pl.BlockSpec((PROJ_ROWS, LANES), lambda i: (i % steps_per_seq, 0)),
            pl.BlockSpec((PROJ_ROWS, LANES), lambda i: (i % steps_per_seq, 0)),
            _resident(cross.shape), _resident(sdec.shape), _resident(dmat.shape),
            _resident(cdec.shape), _resident(gn_gain.shape),
        ],
        out_specs=[pl.BlockSpec((PROJ_ROWS, PROJ_OUT_WIDTH), lambda i: (i, 0)),
                   pl.BlockSpec((PROJ_ROWS, R_V_WIDTH), lambda i: (i, 0))],
        out_shape=[jax.ShapeDtypeStruct((n_tok, PROJ_OUT_WIDTH), BF16),
                   jax.ShapeDtypeStruct((n_tok, R_V_WIDTH), BF16)],
        scratch_shapes=[pltpu.VMEM((n_pairs, LANES, HEADS_PER_LANE_GROUP * R_VAL_DIM), F32),
                        narrow, narrow, narrow, narrow, wide, wide],
        compiler_params=pltpu.CompilerParams(
            dimension_semantics=("arbitrary",), vmem_limit_bytes=VMEM_LIMIT_BYTES),
        name="projection_retention",
    )(x2d, gain, w_bf16, cos_t, sin_t, cross, sdec, dmat, cdec, gn_gain)


def _build_bias_tiles(ring_ref, bias_ref):
    n_keys = KEY_BLOCKS * SEQ_BLOCK
    sublane = lax.broadcasted_iota(jnp.int32, (SUBLANES, BIAS_RING), 0)
    lane = lax.broadcasted_iota(jnp.int32, (SUBLANES, SEQ_BLOCK), 1)
    masked = jnp.full((SUBLANES, SEQ_BLOCK), NEG_INF, F32)
    for h in range(A_HEADS):
        base = ring_ref[h]
        shift = 1
        while shift < SUBLANES:
            base = jnp.where((sublane & shift) != 0, pltpu.roll(base, shift, 1), base)
            shift *= 2
        for a in range(n_keys // SUBLANES):
            key_chunk = a * SUBLANES // CHUNK
            rows = slice(a * SUBLANES, (a + 1) * SUBLANES)
            tile = pltpu.roll(base, a * SUBLANES, 1)[:, :SEQ_BLOCK]
            in_band = ((lane >= (key_chunk - N_LEFT_CHUNKS) * CHUNK)
                       & (lane < (key_chunk + 1) * CHUNK))
            tile = jnp.where(in_band, tile, NEG_INF)
            for v in range(KEY_BLOCKS):
                in_sequence = a * SUBLANES >= (KEY_BLOCKS - 1 - v) * SEQ_BLOCK
                bias_ref[v, h, rows, :] = tile if in_sequence else masked


def _attention_block(sb, q_ref, k_prev_ref, k_cur_ref, v_prev_ref, v_cur_ref, g_ref, bias_ref,
                     s_refs):
    lane = lax.broadcasted_iota(jnp.int32, (1, LANES), 1)
    low_half = lane < A_HEAD_DIM
    halves = (low_half, ~low_half)
    row_half = (lax.broadcasted_iota(jnp.int32, (LANES, 1), 0) < A_HEAD_DIM)
    row_halves = (row_half, ~row_half)
    n_pairs = A_HEADS // HEADS_PER_LANE_GROUP
    rows = slice(sb * SEQ_BLOCK, (sb + 1) * SEQ_BLOCK)
    runtime_zero = jnp.minimum(pl.program_id(1), 0)
    variant = jnp.minimum(pl.program_id(1) * ATT_SUB + sb, KEY_BLOCKS - 1)

    def window(prev_ref, cur_ref, cols):
        blocks = []
        for back in range(KEY_BLOCKS - 1, -1, -1):
            blk = ATT_SUB + sb - back
            ref = prev_ref if blk < ATT_SUB else cur_ref
            start = (blk % ATT_SUB) * SEQ_BLOCK
            blocks.append(ref[start:start + SEQ_BLOCK, cols])
        return jnp.concatenate(blocks, axis=0)

    n_keys = KEY_BLOCKS * SEQ_BLOCK
    chunks_per_lane_group = LANES // CHUNK

    def band_rows(t):
        first_chunk = t * chunks_per_lane_group
        return slice(first_chunk * CHUNK,
                     (first_chunk + chunks_per_lane_group + N_LEFT_CHUNKS) * CHUNK)

    col_max = []
    for p in range(n_pairs):
        cols = slice(p * LANES, (p + 1) * LANES)
        q_pair = q_ref[rows, cols]
        k_pair = window(k_prev_ref, k_cur_ref, cols)
        for e in range(HEADS_PER_LANE_GROUP):
            h = p * HEADS_PER_LANE_GROUP + e
            q_h = jnp.where(halves[e], q_pair, jnp.zeros_like(q_pair))
            s = lax.dot_general(k_pair, q_h, (((1,), (1,)), ((), ())), preferred_element_type=F32)
            maxima = []
            for t in range(SEQ_BLOCK // LANES):
                band, lanes = band_rows(t), slice(t * LANES, (t + 1) * LANES)
                biased = s[band, lanes] + bias_ref[variant, h, band, lanes]
                s_refs[h][0, band, lanes] = biased
                maxima.append(jnp.max(biased, axis=0, keepdims=True))
            col_max.append(maxima)

    def second_pass():
        gated = []
        for p in range(n_pairs):
            cols = slice(p * LANES, (p + 1) * LANES)
            v_t = window(v_prev_ref, v_cur_ref, cols).T
            outs = []
            for e in range(HEADS_PER_LANE_GROUP):
                h = p * HEADS_PER_LANE_GROUP + e
                columns = []
                for t in range(SEQ_BLOCK // LANES):
                    band, lanes = band_rows(t), slice(t * LANES, (t + 1) * LANES)
                    in_band = jnp.exp2(s_refs[h][runtime_zero, band, lanes]
                                       - col_max[h][t]).astype(BF16)
                    pieces = [jnp.zeros((band.start, LANES), BF16)] if band.start else []
                    pieces.append(in_band)
                    if band.stop < n_keys:
                        pieces.append(jnp.zeros((n_keys - band.stop, LANES), BF16))
                    columns.append(jnp.concatenate(pieces, axis=0))
                probs = jnp.concatenate(columns, axis=1)
                v_h = jnp.where(row_halves[e], v_t, jnp.ones_like(v_t))
                acc = jnp.dot(v_h, probs, preferred_element_type=F32)
                own = slice(e * A_HEAD_DIM, (e + 1) * A_HEAD_DIM)
                other = slice((1 - e) * A_HEAD_DIM, (2 - e) * A_HEAD_DIM)
                outs.append(acc[own] / acc[other])
            o_pair = jnp.concatenate(outs, axis=0).T
            g = g_ref[rows, cols].astype(F32)
            gated.append((o_pair * _silu(g)).astype(BF16))
        return jnp.concatenate(gated, axis=1)

    return second_pass


def _attn_out_kernel(q_ref, k_prev_ref, k_cur_ref, v_prev_ref, v_cur_ref, g_ref, ring_ref,
                     ret_ref, ma_ref, mr_ref, x_ref, wa_ref, wr_ref, wo_ref, fg_ref, o_ref,
                     bias_ref, *s_refs):
    @pl.when((pl.program_id(0) == 0) & (pl.program_id(1) == 0))
    def _():
        _build_bias_tiles(ring_ref, bias_ref)

    second_passes = [
        _attention_block(sb, q_ref, k_prev_ref, k_cur_ref, v_prev_ref, v_cur_ref, g_ref, bias_ref,
                         s_refs[sb * A_HEADS:(sb + 1) * A_HEADS])
        for sb in range(ATT_SUB)]
    for sb in range(ATT_SUB):
        rows = slice(sb * SEQ_BLOCK, (sb + 1) * SEQ_BLOCK)
        att = second_passes[sb]()
        branch_a = jnp.dot(att, wa_ref[...], preferred_element_type=F32)
        branch_r = jnp.dot(ret_ref[rows, :], wr_ref[...], preferred_element_type=F32)
        mixed = (_sigmoid(ma_ref[rows, :].astype(F32)) * branch_a
                 + _sigmoid(mr_ref[rows, :].astype(F32)) * branch_r)
        y = x_ref[rows, :] + jnp.dot(mixed.astype(BF16), wo_ref[...], preferred_element_type=F32)
        inv = lax.rsqrt(jnp.mean(y * y, axis=-1, keepdims=True) + NORM_EPS)
        o_ref[rows, :] = y * inv * fg_ref[...]


def _attention_and_output(proj, ret, x2d, bias_ring, wa, wr, wo, final_gain, batch, seq_len):
    assert ATT_SUB >= KEY_BLOCKS - 1, "the key window must fit in the previous and current step"
    n_keys = KEY_BLOCKS * SEQ_BLOCK
    n_tok = proj.shape[0]
    step_rows = ATT_SUB * SEQ_BLOCK
    steps_per_seq = seq_len // step_rows

    def rows(b, j):
        return b * steps_per_seq + j

    def cur(col, width=COL_BLOCK):
        return pl.BlockSpec((step_rows, width), lambda b, j: (rows(b, j), col))

    def prev(col):
        return pl.BlockSpec((step_rows, COL_BLOCK),
                            lambda b, j: (rows(b, jnp.maximum(j - 1, 0)), col))

    return pl.pallas_call(
        _attn_out_kernel,
        grid=(batch, steps_per_seq),
        in_specs=[cur(COL_Q_A), prev(COL_K_A), cur(COL_K_A), prev(COL_V_A), cur(COL_V_A),
                  cur(COL_G_A), _resident(bias_ring.shape),
                  cur(0, R_V_WIDTH), cur(WIDE_M_A, D_MODEL), cur(WIDE_M_R, D_MODEL),
                  cur(0, D_MODEL),
                  _resident(wa.shape), _resident(wr.shape), _resident(wo.shape),
                  _resident(final_gain.shape)],
        out_specs=cur(0, D_MODEL),
        out_shape=jax.ShapeDtypeStruct((n_tok, D_MODEL), F32),
        scratch_shapes=[pltpu.VMEM((KEY_BLOCKS, A_HEADS, n_keys, SEQ_BLOCK), F32)]
        + [pltpu.VMEM((1, n_keys, SEQ_BLOCK), F32)] * (ATT_SUB * A_HEADS),
        compiler_params=pltpu.CompilerParams(
            dimension_semantics=("arbitrary", "arbitrary"), vmem_limit_bytes=VMEM_LIMIT_BYTES),
        name="attention_output",
    )(proj, proj, proj, proj, proj, proj, bias_ring, ret, proj, proj, x2d, wa, wr, wo, final_gain)


def _rotary_tables(seq_len):
    half = R_KEY_DIM // 2
    inv_freq = jnp.power(ROPE_BASE, -jnp.arange(half, dtype=F32) / half)
    ang = jnp.arange(seq_len).astype(F32)[:, None] * inv_freq[None, :]
    cos, sin = jnp.cos(ang), jnp.sin(ang)
    cos_t = jnp.tile(jnp.concatenate([cos, cos], axis=-1), (1, HEADS_PER_LANE_GROUP))
    sin_t = jnp.tile(jnp.concatenate([-sin, sin], axis=-1), (1, HEADS_PER_LANE_GROUP))
    return cos_t, sin_t


def _take_columns(table, idx):
    pieces, start = [], 0
    while start < len(idx):
        end = start + 1
        if end < len(idx) and idx[end] == idx[start]:
            while end < len(idx) and idx[end] == idx[start]:
                end += 1
            piece = jnp.broadcast_to(table[:, idx[start]:idx[start] + 1], (table.shape[0], end - start))
        else:
            while end < len(idx) and idx[end] == idx[end - 1] + 1:
                end += 1
            piece = table[:, idx[start]:idx[end - 1] + 1]
        pieces.append(piece)
        start = end
    return jnp.concatenate(pieces, axis=1)


def _bias_ring(rel_bias):
    n_keys = KEY_BLOCKS * SEQ_BLOCK
    assert n_keys + SEQ_BLOCK - 1 <= BIAS_RING
    u = np.arange(BIAS_RING)
    t = np.where(u < SEQ_BLOCK, u, u - BIAS_RING)
    idx = np.clip((KEY_BLOCKS - 1) * SEQ_BLOCK + t, -REL_CLIP, REL_CLIP) + REL_CLIP
    ring = _take_columns(rel_bias.astype(F32), idx) * LOG2_E
    return jnp.broadcast_to(ring[:, None, :], (rel_bias.shape[0], SUBLANES, BIAS_RING))


def _retention_tables():
    n_pairs = R_HEADS // HEADS_PER_LANE_GROUP
    log_gamma = jnp.log1p(-jnp.exp2(-5.0 - jnp.arange(R_HEADS, dtype=F32)))
    pos = jnp.arange(RET_BLOCK, dtype=F32)
    diff = pos[:, None] - pos[None, :]
    chunk_diff = (np.arange(RET_BLOCK)[:, None] // CHUNK) - (np.arange(RET_BLOCK)[None, :] // CHUNK)
    expo = jnp.where(chunk_diff == 0, jnp.abs(diff), diff)
    dmat = jnp.where((chunk_diff >= 0)[None], jnp.exp(expo[None] * log_gamma[:, None, None]), 0.0)
    dmat = dmat.reshape(n_pairs, HEADS_PER_LANE_GROUP, RET_BLOCK, RET_BLOCK)
    dmat = jnp.concatenate([dmat[:, e] for e in range(HEADS_PER_LANE_GROUP)], axis=-1)
    per_lane = lambda tab, width: jnp.tile(jnp.repeat(tab.T, width, axis=1), (SEQ_BLOCK // RET_BLOCK, 1))
    cross = per_lane(jnp.exp((pos + 1.0)[None, :] * log_gamma[:, None]), R_KEY_DIM)
    sdec = per_lane(jnp.exp((RET_BLOCK - 1.0 - pos)[None, :] * log_gamma[:, None]), R_KEY_DIM)
    block_decay = jnp.exp(RET_BLOCK * log_gamma)
    cdec = jnp.broadcast_to(jnp.repeat(block_decay, R_KEY_DIM).reshape(n_pairs, LANES, 1),
                            (n_pairs, LANES, HEADS_PER_LANE_GROUP * R_VAL_DIM))
    return dmat.astype(F32), cross, sdec, cdec


def kernel(x, norm_gain, w_in, rel_bias, gn_gain, w_out_attn, w_out_ret, w_out, final_gain):
    batch, seq_len, d_model = x.shape
    assert d_model == D_MODEL and norm_gain.shape[0] == 1, "single-layer block of width D_MODEL"
    assert seq_len % PROJ_ROWS == 0 and PROJ_ROWS % SEQ_BLOCK == 0
    assert seq_len % (ATT_SUB * SEQ_BLOCK) == 0
    x2d = x.reshape(batch * seq_len, D_MODEL)
    cos_t, sin_t = _rotary_tables(seq_len)
    proj, ret = _projection_and_retention(x2d, norm_gain[0][None, :], w_in[0].astype(BF16), cos_t, sin_t,
                                          _retention_tables(), gn_gain[0][None, :], seq_len)
    out = _attention_and_output(proj, ret, x2d, _bias_ring(rel_bias[0]), w_out_attn[0].astype(BF16),
                                w_out_ret[0].astype(BF16), w_out[0].astype(BF16),
                                final_gain[None, :], batch, seq_len)
    return out.reshape(batch, seq_len, D_MODEL)
```

```python
import functools

import numpy as np
import jax
import jax.numpy as jnp
from jax import lax
from jax.experimental import pallas as pl
from jax.experimental.pallas import tpu as pltpu

D_MODEL = 1024
CHUNK = 64
N_LEFT_CHUNKS = 8
A_HEADS = 8
A_HEAD_DIM = 64
A_WIDTH = A_HEADS * A_HEAD_DIM
REL_CLIP = 128
R_HEADS = 8
R_KEY_DIM = 64
R_VAL_DIM = 128
R_QK_WIDTH = R_HEADS * R_KEY_DIM
R_V_WIDTH = R_HEADS * R_VAL_DIM
ROPE_BASE = 10000.0
NORM_EPS = 1e-6
GN_EPS = 1e-5
NEG_INF = -1e30
LOG2_E = 1.4426950408889634
IN_WIDTH = 4 * A_WIDTH + 2 * R_QK_WIDTH + 2 * R_V_WIDTH + 2 * D_MODEL

COL_BLOCK = 512
N_COL_BLOCKS = IN_WIDTH // COL_BLOCK
W_Q_A, W_K_A, W_V_A, W_G_A, W_Q_R, W_K_R = 0, 1, 2, 3, 4, 5
W_V_R, W_G_R, W_M_A, W_M_R = (6, 7), (8, 9), (10, 11), (12, 13)
RETENTION_W_BLOCKS = (W_Q_R, W_K_R) + W_V_R + W_G_R
COL_Q_A, COL_K_A, COL_V_A, COL_G_A = 0, 1, 2, 3
STORED_BLOCK = {W_Q_A: COL_Q_A, W_K_A: COL_K_A, W_V_A: COL_V_A, W_G_A: COL_G_A,
                W_M_A[0]: 4, W_M_A[1]: 5, W_M_R[0]: 6, W_M_R[1]: 7}
PROJ_OUT_WIDTH = len(STORED_BLOCK) * COL_BLOCK
WIDE_M_A, WIDE_M_R = 2, 3

LANES = 128
SUBLANES = 8
HEADS_PER_LANE_GROUP = LANES // A_HEAD_DIM

PROJ_ROWS = 512
SEQ_BLOCK = 256
RET_BLOCK = 128
CHUNKS_PER_BLOCK = SEQ_BLOCK // CHUNK
ATT_SUB = 2
KEY_BLOCKS = N_LEFT_CHUNKS // CHUNKS_PER_BLOCK + 1
BIAS_RING = (KEY_BLOCKS + 1) * SEQ_BLOCK
VMEM_LIMIT_BYTES = 60 * 1024 * 1024

BF16 = jnp.bfloat16
F32 = jnp.float32


def _resident(shape):
    zeros = (0,) * len(shape)
    return pl.BlockSpec(shape, lambda *_: zeros, pipeline_mode=pl.Buffered(1))


def _sigmoid(x):
    return 0.5 * jnp.tanh(0.5 * x) + 0.5


def _silu(g):
    half = 0.5 * g
    return half * jnp.tanh(half) + half


def _retention_block(r, sub, out_rows, q_scr, k_scr, qc_scr, kd_scr, v_scr, g_scr, dmat_ref, cdec_ref,
                     gain_ref, o_ref, states):
    pair_lanes = HEADS_PER_LANE_GROUP * R_VAL_DIM
    low_half = lax.broadcasted_iota(jnp.int32, (1, LANES), 1) < R_KEY_DIM
    even_lanes = lax.broadcasted_iota(jnp.int32, (1, pair_lanes), 1) < R_VAL_DIM
    own_block = ((lax.broadcasted_iota(jnp.int32, (LANES, pair_lanes), 0) < R_KEY_DIM)
                 == (lax.broadcasted_iota(jnp.int32, (LANES, pair_lanes), 1) < R_VAL_DIM))
    new_states = []
    for p in range(R_HEADS // HEADS_PER_LANE_GROUP):
        cols = slice(p * LANES, (p + 1) * LANES)
        vcols = slice(p * pair_lanes, (p + 1) * pair_lanes)
        q_pair = q_scr[r, sub, cols]
        k_pair = k_scr[r, sub, cols]
        v_pair = v_scr[r, sub, vcols]
        k_zero = jnp.zeros_like(k_pair)
        k_split = jnp.concatenate([jnp.where(low_half, k_pair, k_zero),
                                   jnp.where(low_half, k_zero, k_pair)], axis=0)
        s = lax.dot_general(q_pair, k_split, (((1,), (1,)), ((), ())),
                            preferred_element_type=F32) * dmat_ref[p]
        v_zero = jnp.zeros_like(v_pair)
        v_diag = jnp.concatenate([jnp.where(even_lanes, v_pair, v_zero),
                                  jnp.where(even_lanes, v_zero, v_pair)], axis=0)
        lhs = jnp.concatenate([s.astype(BF16), qc_scr[r, sub, cols]], axis=1)
        rhs = jnp.concatenate([v_diag, states[p].astype(BF16)], axis=0)
        o_pair = jnp.dot(lhs, rhs, preferred_element_type=F32)
        kv = lax.dot_general(kd_scr[r, sub, cols], v_pair, (((0,), (0,)), ((), ())),
                             preferred_element_type=F32)
        new_states.append(states[p] * cdec_ref[p] + jnp.where(own_block, kv, 0.0))
        for e in range(HEADS_PER_LANE_GROUP):
            h = p * HEADS_PER_LANE_GROUP + e
            hcols = slice(h * R_VAL_DIM, (h + 1) * R_VAL_DIM)
            o = o_pair[:, e * R_VAL_DIM:(e + 1) * R_VAL_DIM]
            mean = jnp.mean(o, axis=-1, keepdims=True)
            cen = o - mean
            var = jnp.mean(cen * cen, axis=-1, keepdims=True)
            normed = cen * lax.rsqrt(var + GN_EPS) * gain_ref[:, hcols]
            g = g_scr[r, sub, hcols].astype(F32)
            o_ref[out_rows, hcols] = (normed * _silu(g)).astype(BF16)
    return new_states


def _proj_kernel(steps_per_seq, x_ref, gain_ref, w_ref, cos_ref, sin_ref, cross_ref, sdec_ref,
                 dmat_ref, cdec_ref, gn_gain_ref, o_ref, ret_ref, state_ref,
                 q_scr, k_scr, qc_scr, kd_scr, v_scr, g_scr):
    @pl.when(pl.program_id(0) % steps_per_seq == 0)
    def _():
        state_ref[...] = jnp.zeros_like(state_ref)

    lane = lax.broadcasted_iota(jnp.int32, (1, COL_BLOCK), 1)
    first_half = (lane % R_KEY_DIM) < (R_KEY_DIM // 2)
    groups = COL_BLOCK // LANES
    n_pairs = R_HEADS // HEADS_PER_LANE_GROUP
    states = [state_ref[p] for p in range(n_pairs)]
    for r in range(PROJ_ROWS // SEQ_BLOCK):
        rows = slice(r * SEQ_BLOCK, (r + 1) * SEQ_BLOCK)
        x = x_ref[rows, :]
        inv = lax.rsqrt(jnp.mean(x * x, axis=-1, keepdims=True) + NORM_EPS)
        h = (x * inv * gain_ref[...]).astype(BF16)

        for c in RETENTION_W_BLOCKS:
            acc = jnp.dot(h, w_ref[:, _block_cols(c)], preferred_element_type=F32)
            if c in (W_Q_R, W_K_R):
                partner = jnp.where(first_half,
                                    pltpu.roll(acc, COL_BLOCK - R_KEY_DIM // 2, 1),
                                    pltpu.roll(acc, R_KEY_DIM // 2, 1))
                cos = jnp.concatenate([cos_ref[rows, :]] * groups, axis=1)
                sin = jnp.concatenate([sin_ref[rows, :]] * groups, axis=1)
                acc = acc * cos + partner * sin
            if c == W_Q_R:
                acc = acc * (R_KEY_DIM ** -0.5)
                q_scr[r] = acc.astype(BF16)
                qc_scr[r] = (acc * cross_ref[...]).astype(BF16)
            elif c == W_K_R:
                k_scr[r] = acc.astype(BF16)
                kd_scr[r] = (acc * sdec_ref[...]).astype(BF16)
            elif c in W_V_R:
                v_scr[r, :, _block_cols(W_V_R.index(c))] = acc.astype(BF16)
            else:
                g_scr[r, :, _block_cols(W_G_R.index(c))] = acc.astype(BF16)
        for b in range(SEQ_BLOCK // RET_BLOCK):
            sub = slice(b * RET_BLOCK, (b + 1) * RET_BLOCK)
            out_rows = slice(r * SEQ_BLOCK + b * RET_BLOCK, r * SEQ_BLOCK + (b + 1) * RET_BLOCK)
            states = _retention_block(r, sub, out_rows, q_scr, k_scr, qc_scr, kd_scr, v_scr, g_scr,
                                      dmat_ref, cdec_ref, gn_gain_ref, ret_ref, states)

        for c, stored in STORED_BLOCK.items():
            acc = jnp.dot(h, w_ref[:, _block_cols(c)], preferred_element_type=F32)
            if c == W_Q_A:
                acc = acc * (A_HEAD_DIM ** -0.5 * LOG2_E)
            o_ref[rows, _block_cols(stored)] = acc.astype(BF16)
    for p in range(n_pairs):
        state_ref[p] = states[p]


def _block_cols(c):
    return slice(c * COL_BLOCK, (c + 1) * COL_BLOCK)


def _projection_and_retention(x2d, gain, w_bf16, cos_t, sin_t, tables, gn_gain, seq_len):
    dmat, cross, sdec, cdec = tables
    n_tok = x2d.shape[0]
    steps_per_seq = seq_len // PROJ_ROWS
    row_groups = PROJ_ROWS // SEQ_BLOCK
    n_pairs = R_HEADS // HEADS_PER_LANE_GROUP
    narrow = pltpu.VMEM((row_groups, SEQ_BLOCK, R_QK_WIDTH), BF16)
    wide = pltpu.VMEM((row_groups, SEQ_BLOCK, R_V_WIDTH), BF16)
    return pl.pallas_call(
        functools.partial(_proj_kernel, steps_per_seq),
        grid=(n_tok // PROJ_ROWS,),
        in_specs=[
            pl.BlockSpec((PROJ_ROWS, D_MODEL), lambda i: (i, 0)),
            _resident((1, D_MODEL)),
            _resident((D_MODEL, IN_WIDTH)),
            pl.BlockSpec((PROJ_ROWS, LANES), lambda i: (i % steps_per_seq, 0)),
            pl.BlockSpec((PROJ_ROWS, LANES), lambda i: (i % steps_per_seq, 0)),
            _resident(cross.shape), _resident(sdec.shape), _resident(dmat.shape),
            _resident(cdec.shape), _resident(gn_gain.shape),
        ],
        out_specs=[pl.BlockSpec((PROJ_ROWS, PROJ_OUT_WIDTH), lambda i: (i, 0)),
                   pl.BlockSpec((PROJ_ROWS, R_V_WIDTH), lambda i: (i, 0))],
        out_shape=[jax.ShapeDtypeStruct((n_tok, PROJ_OUT_WIDTH), BF16),
                   jax.ShapeDtypeStruct((n_tok, R_V_WIDTH), BF16)],
        scratch_shapes=[pltpu.VMEM((n_pairs, LANES, HEADS_PER_LANE_GROUP * R_VAL_DIM), F32),
                        narrow, narrow, narrow, narrow, wide, wide],
        compiler_params=pltpu.CompilerParams(
            dimension_semantics=("arbitrary",), vmem_limit_bytes=VMEM_LIMIT_BYTES),
        name="projection_retention",
    )(x2d, gain, w_bf16, cos_t, sin_t, cross, sdec, dmat, cdec, gn_gain)


def _build_bias_tiles(ring_ref, bias_ref):
    n_keys = KEY_BLOCKS * SEQ_BLOCK
    sublane = lax.broadcasted_iota(jnp.int32, (SUBLANES, BIAS_RING), 0)
    lane = lax.broadcasted_iota(jnp.int32, (SUBLANES, SEQ_BLOCK), 1)
    masked = jnp.full((SUBLANES, SEQ_BLOCK), NEG_INF, F32)
    for h in range(A_HEADS):
        base = ring_ref[h]
        shift = 1
        while shift < SUBLANES:
            base = jnp.where((sublane & shift) != 0, pltpu.roll(base, shift, 1), base)
            shift *= 2
        for a in range(n_keys // SUBLANES):
            key_chunk = a * SUBLANES // CHUNK
            rows = slice(a * SUBLANES, (a + 1) * SUBLANES)
            tile = pltpu.roll(base, a * SUBLANES, 1)[:, :SEQ_BLOCK]
            in_band = ((lane >= (key_chunk - N_LEFT_CHUNKS) * CHUNK)
                       & (lane < (key_chunk + 1) * CHUNK))
            tile = jnp.where(in_band, tile, NEG_INF)
            for v in range(KEY_BLOCKS):
                in_sequence = a * SUBLANES >= (KEY_BLOCKS - 1 - v) * SEQ_BLOCK
                bias_ref[v, h, rows, :] = tile if in_sequence else masked


def _attention_block(sb, q_ref, k_prev_ref, k_cur_ref, v_prev_ref, v_cur_ref, g_ref, bias_ref,
                     s_refs):
    lane = lax.broadcasted_iota(jnp.int32, (1, LANES), 1)
    low_half = lane < A_HEAD_DIM
    halves = (low_half, ~low_half)
    row_half = (lax.broadcasted_iota(jnp.int32, (LANES, 1), 0) < A_HEAD_DIM)
    row_halves = (row_half, ~row_half)
    n_pairs = A_HEADS // HEADS_PER_LANE_GROUP
    rows = slice(sb * SEQ_BLOCK, (sb + 1) * SEQ_BLOCK)
    runtime_zero = jnp.minimum(pl.program_id(1), 0)
    variant = jnp.minimum(pl.program_id(1) * ATT_SUB + sb, KEY_BLOCKS - 1)

    def window(prev_ref, cur_ref, cols):
        blocks = []
        for back in range(KEY_BLOCKS - 1, -1, -1):
            blk = ATT_SUB + sb - back
            ref = prev_ref if blk < ATT_SUB else cur_ref
            start = (blk % ATT_SUB) * SEQ_BLOCK
            blocks.append(ref[start:start + SEQ_BLOCK, cols])
        return jnp.concatenate(blocks, axis=0)

    n_keys = KEY_BLOCKS * SEQ_BLOCK
    chunks_per_lane_group = LANES // CHUNK

    def band_rows(t):
        first_chunk = t * chunks_per_lane_group
        return slice(first_chunk * CHUNK,
                     (first_chunk + chunks_per_lane_group + N_LEFT_CHUNKS) * CHUNK)

    col_max = []
    for p in range(n_pairs):
        cols = slice(p * LANES, (p + 1) * LANES)
        q_pair = q_ref[rows, cols]
        k_pair = window(k_prev_ref, k_cur_ref, cols)
        for e in range(HEADS_PER_LANE_GROUP):
            h = p * HEADS_PER_LANE_GROUP + e
            q_h = jnp.where(halves[e], q_pair, jnp.zeros_like(q_pair))
            s = lax.dot_general(k_pair, q_h, (((1,), (1,)), ((), ())), preferred_element_type=F32)
            maxima = []
            for t in range(SEQ_BLOCK // LANES):
                band, lanes = band_rows(t), slice(t * LANES, (t + 1) * LANES)
                biased = s[band, lanes] + bias_ref[variant, h, band, lanes]
                s_refs[h][0, band, lanes] = biased
                maxima.append(jnp.max(biased, axis=0, keepdims=True))
            col_max.append(maxima)

    def second_pass():
        gated = []
        for p in range(n_pairs):
            cols = slice(p * LANES, (p + 1) * LANES)
            v_t = window(v_prev_ref, v_cur_ref, cols).T
            outs = []
            for e in range(HEADS_PER_LANE_GROUP):
                h = p * HEADS_PER_LANE_GROUP + e
                columns = []
                for t in range(SEQ_BLOCK // LANES):
                    band, lanes = band_rows(t), slice(t * LANES, (t + 1) * LANES)
                    in_band = jnp.exp2(s_refs[h][runtime_zero, band, lanes]
                                       - col_max[h][t]).astype(BF16)
                    pieces = [jnp.zeros((band.start, LANES), BF16)] if band.start else []
                    pieces.append(in_band)
                    if band.stop < n_keys:
                        pieces.append(jnp.zeros((n_keys - band.stop, LANES), BF16))
                    columns.append(jnp.concatenate(pieces, axis=0))
                probs = jnp.concatenate(columns, axis=1)
                v_h = jnp.where(row_halves[e], v_t, jnp.ones_like(v_t))
                acc = jnp.dot(v_h, probs, preferred_element_type=F32)
                own = slice(e * A_HEAD_DIM, (e + 1) * A_HEAD_DIM)
                other = slice((1 - e) * A_HEAD_DIM, (2 - e) * A_HEAD_DIM)
                outs.append(acc[own] / acc[other])
            o_pair = jnp.concatenate(outs, axis=0).T
            g = g_ref[rows, cols].astype(F32)
            gated.append((o_pair * _silu(g)).astype(BF16))
        return jnp.concatenate(gated, axis=1)

    return second_pass


def _attn_out_kernel(q_ref, k_prev_ref, k_cur_ref, v_prev_ref, v_cur_ref, g_ref, ring_ref,
                     ret_ref, ma_ref, mr_ref, x_ref, wa_ref, wr_ref, wo_ref, fg_ref, o_ref,
                     bias_ref, *s_refs):
    @pl.when((pl.program_id(0) == 0) & (pl.program_id(1) == 0))
    def _():
        _build_bias_tiles(ring_ref, bias_ref)

    second_passes = [
        _attention_block(sb, q_ref, k_prev_ref, k_cur_ref, v_prev_ref, v_cur_ref, g_ref, bias_ref,
                         s_refs[sb * A_HEADS:(sb + 1) * A_HEADS])
        for sb in range(ATT_SUB)]
    att = jnp.concatenate([second_pass() for second_pass in second_passes], axis=0)
    branch_a = jnp.dot(att, wa_ref[...], preferred_element_type=F32)
    branch_r = jnp.dot(ret_ref[...], wr_ref[...], preferred_element_type=F32)
    mixed = (_sigmoid(ma_ref[...].astype(F32)) * branch_a
             + _sigmoid(mr_ref[...].astype(F32)) * branch_r)
    y = x_ref[...] + jnp.dot(mixed.astype(BF16), wo_ref[...], preferred_element_type=F32)
    inv = lax.rsqrt(jnp.mean(y * y, axis=-1, keepdims=True) + NORM_EPS)
    o_ref[...] = y * inv * fg_ref[...]


def _attention_and_output(proj, ret, x2d, bias_ring, wa, wr, wo, final_gain, batch, seq_len):
    assert ATT_SUB >= KEY_BLOCKS - 1, "the key window must fit in the previous and current step"
    n_keys = KEY_BLOCKS * SEQ_BLOCK
    n_tok = proj.shape[0]
    step_rows = ATT_SUB * SEQ_BLOCK
    steps_per_seq = seq_len // step_rows

    def rows(b, j):
        return b * steps_per_seq + j

    def cur(col, width=COL_BLOCK):
        return pl.BlockSpec((step_rows, width), lambda b, j: (rows(b, j), col))

    def prev(col):
        return pl.BlockSpec((step_rows, COL_BLOCK),
                            lambda b, j: (rows(b, jnp.maximum(j - 1, 0)), col))

    return pl.pallas_call(
        _attn_out_kernel,
        grid=(batch, steps_per_seq),
        in_specs=[cur(COL_Q_A), prev(COL_K_A), cur(COL_K_A), prev(COL_V_A), cur(COL_V_A),
                  cur(COL_G_A), _resident(bias_ring.shape),
                  cur(0, R_V_WIDTH), cur(WIDE_M_A, D_MODEL), cur(WIDE_M_R, D_MODEL),
                  cur(0, D_MODEL),
                  _resident(wa.shape), _resident(wr.shape), _resident(wo.shape),
                  _resident(final_gain.shape)],
        out_specs=cur(0, D_MODEL),
        out_shape=jax.ShapeDtypeStruct((n_tok, D_MODEL), F32),
        scratch_shapes=[pltpu.VMEM((KEY_BLOCKS, A_HEADS, n_keys, SEQ_BLOCK), F32)]
        + [pltpu.VMEM((1, n_keys, SEQ_BLOCK), F32)] * (ATT_SUB * A_HEADS),
        compiler_params=pltpu.CompilerParams(
            dimension_semantics=("arbitrary", "arbitrary"), vmem_limit_bytes=VMEM_LIMIT_BYTES),
        name="attention_output",
    )(proj, proj, proj, proj, proj, proj, bias_ring, ret, proj, proj, x2d, wa, wr, wo, final_gain)


def _rotary_tables(seq_len):
    half = R_KEY_DIM // 2
    inv_freq = jnp.power(ROPE_BASE, -jnp.arange(half, dtype=F32) / half)
    ang = jnp.arange(seq_len).astype(F32)[:, None] * inv_freq[None, :]
    cos, sin = jnp.cos(ang), jnp.sin(ang)
    cos_t = jnp.tile(jnp.concatenate([cos, cos], axis=-1), (1, HEADS_PER_LANE_GROUP))
    sin_t = jnp.tile(jnp.concatenate([-sin, sin], axis=-1), (1, HEADS_PER_LANE_GROUP))
    return cos_t, sin_t


def _take_columns(table, idx):
    pieces, start = [], 0
    while start < len(idx):
        end = start + 1
        if end < len(idx) and idx[end] == idx[start]:
            while end < len(idx) and idx[end] == idx[start]:
                end += 1
            piece = jnp.broadcast_to(table[:, idx[start]:idx[start] + 1], (table.shape[0], end - start))
        else:
            while end < len(idx) and idx[end] == idx[end - 1] + 1:
                end += 1
            piece = table[:, idx[start]:idx[end - 1] + 1]
        pieces.append(piece)
        start = end
    return jnp.concatenate(pieces, axis=1)


def _bias_ring(rel_bias):
    n_keys = KEY_BLOCKS * SEQ_BLOCK
    assert n_keys + SEQ_BLOCK - 1 <= BIAS_RING
    u = np.arange(BIAS_RING)
    t = np.where(u < SEQ_BLOCK, u, u - BIAS_RING)
    idx = np.clip((KEY_BLOCKS - 1) * SEQ_BLOCK + t, -REL_CLIP, REL_CLIP) + REL_CLIP
    ring = _take_columns(rel_bias.astype(F32), idx) * LOG2_E
    return jnp.broadcast_to(ring[:, None, :], (rel_bias.shape[0], SUBLANES, BIAS_RING))


def _retention_tables():
    n_pairs = R_HEADS // HEADS_PER_LANE_GROUP
    log_gamma = jnp.log1p(-jnp.exp2(-5.0 - jnp.arange(R_HEADS, dtype=F32)))
    pos = jnp.arange(RET_BLOCK, dtype=F32)
    diff = pos[:, None] - pos[None, :]
    chunk_diff = (np.arange(RET_BLOCK)[:, None] // CHUNK) - (np.arange(RET_BLOCK)[None, :] // CHUNK)
    expo = jnp.where(chunk_diff == 0, jnp.abs(diff), diff)
    dmat = jnp.where((chunk_diff >= 0)[None], jnp.exp(expo[None] * log_gamma[:, None, None]), 0.0)
    dmat = dmat.reshape(n_pairs, HEADS_PER_LANE_GROUP, RET_BLOCK, RET_BLOCK)
    dmat = jnp.concatenate([dmat[:, e] for e in range(HEADS_PER_LANE_GROUP)], axis=-1)
    per_lane = lambda tab, width: jnp.tile(jnp.repeat(tab.T, width, axis=1), (SEQ_BLOCK // RET_BLOCK, 1))
    cross = per_lane(jnp.exp((pos + 1.0)[None, :] * log_gamma[:, None]), R_KEY_DIM)
    sdec = per_lane(jnp.exp((RET_BLOCK - 1.0 - pos)[None, :] * log_gamma[:, None]), R_KEY_DIM)
    block_decay = jnp.exp(RET_BLOCK * log_gamma)
    cdec = jnp.broadcast_to(jnp.repeat(block_decay, R_KEY_DIM).reshape(n_pairs, LANES, 1),
                            (n_pairs, LANES, HEADS_PER_LANE_GROUP * R_VAL_DIM))
    return dmat.astype(F32), cross, sdec, cdec


def kernel(x, norm_gain, w_in, rel_bias, gn_gain, w_out_attn, w_out_ret, w_out, final_gain):
    batch, seq_len, d_model = x.shape
    assert d_model == D_MODEL and norm_gain.shape[0] == 1, "single-layer block of width D_MODEL"
    assert seq_len % PROJ_ROWS == 0 and PROJ_ROWS % SEQ_BLOCK == 0
    assert seq_len % (ATT_SUB * SEQ_BLOCK) == 0
    x2d = x.reshape(batch * seq_len, D_MODEL)
    cos_t, sin_t = _rotary_tables(seq_len)
    proj, ret = _projection_and_retention(x2d, norm_gain[0][None, :], w_in[0].astype(BF16), cos_t, sin_t,
                                          _retention_tables(), gn_gain[0][None, :], seq_len)
    out = _attention_and_output(proj, ret, x2d, _bias_ring(rel_bias[0]), w_out_attn[0].astype(BF16),
                                w_out_ret[0].astype(BF16), w_out[0].astype(BF16),
                                final_gain[None, :], batch, seq_len)
    return out.reshape(batch, seq_len, D_MODEL)
```

```python
import functools

import numpy as np
import jax
import jax.numpy as jnp
from jax import lax
from jax.experimental import pallas as pl
from jax.experimental.pallas import tpu as pltpu

D_MODEL = 1024
CHUNK = 64
N_LEFT_CHUNKS = 8
A_HEADS = 8
A_HEAD_DIM = 64
A_WIDTH = A_HEADS * A_HEAD_DIM
REL_CLIP = 128
R_HEADS = 8
R_KEY_DIM = 64
R_VAL_DIM = 128
R_QK_WIDTH = R_HEADS * R_KEY_DIM
R_V_WIDTH = R_HEADS * R_VAL_DIM
ROPE_BASE = 10000.0
NORM_EPS = 1e-6
GN_EPS = 1e-5
NEG_INF = -1e30
LOG2_E = 1.4426950408889634
IN_WIDTH = 4 * A_WIDTH + 2 * R_QK_WIDTH + 2 * R_V_WIDTH + 2 * D_MODEL

COL_BLOCK = 512
N_COL_BLOCKS = IN_WIDTH // COL_BLOCK
W_Q_A, W_K_A, W_V_A, W_G_A, W_Q_R, W_K_R = 0, 1, 2, 3, 4, 5
W_V_R, W_G_R, W_M_A, W_M_R = (6, 7), (8, 9), (10, 11), (12, 13)
RETENTION_W_BLOCKS = (W_Q_R, W_K_R) + W_V_R + W_G_R
COL_Q_A, COL_K_A, COL_V_A, COL_G_A = 0, 1, 2, 3
STORED_BLOCK = {W_Q_A: COL_Q_A, W_K_A: COL_K_A, W_V_A: COL_V_A, W_G_A: COL_G_A,
                W_M_A[0]: 4, W_M_A[1]: 5, W_M_R[0]: 6, W_M_R[1]: 7}
PROJ_OUT_WIDTH = len(STORED_BLOCK) * COL_BLOCK
WIDE_M_A, WIDE_M_R = 2, 3

LANES = 128
SUBLANES = 8
HEADS_PER_LANE_GROUP = LANES // A_HEAD_DIM

PROJ_ROWS = 512
SEQ_BLOCK = 256
RET_BLOCK = 128
CHUNKS_PER_BLOCK = SEQ_BLOCK // CHUNK
ATT_SUB = 2
KEY_BLOCKS = N_LEFT_CHUNKS // CHUNKS_PER_BLOCK + 1
BIAS_RING = (KEY_BLOCKS + 1) * SEQ_BLOCK
VMEM_LIMIT_BYTES = 60 * 1024 * 1024

BF16 = jnp.bfloat16
F32 = jnp.float32


def _resident(shape):
    zeros = (0,) * len(shape)
    return pl.BlockSpec(shape, lambda *_: zeros, pipeline_mode=pl.Buffered(1))


def _sigmoid(x):
    return 0.5 * jnp.tanh(0.5 * x) + 0.5


def _silu(g):
    half = 0.5 * g
    return half * jnp.tanh(half) + half


def _retention_block(r, sub, out_rows, q_scr, k_scr, qc_scr, kd_scr, v_scr, g_scr, dmat_ref, cdec_ref,
                     gain_ref, o_ref, states):
    pair_lanes = HEADS_PER_LANE_GROUP * R_VAL_DIM
    low_half = lax.broadcasted_iota(jnp.int32, (1, LANES), 1) < R_KEY_DIM
    even_lanes = lax.broadcasted_iota(jnp.int32, (1, pair_lanes), 1) < R_VAL_DIM
    own_block = ((lax.broadcasted_iota(jnp.int32, (LANES, pair_lanes), 0) < R_KEY_DIM)
                 == (lax.broadcasted_iota(jnp.int32, (LANES, pair_lanes), 1) < R_VAL_DIM))
    new_states = []
    for p in range(R_HEADS // HEADS_PER_LANE_GROUP):
        cols = slice(p * LANES, (p + 1) * LANES)
        vcols = slice(p * pair_lanes, (p + 1) * pair_lanes)
        q_pair = q_scr[r, sub, cols]
        k_pair = k_scr[r, sub, cols]
        v_pair = v_scr[r, sub, vcols]
        k_zero = jnp.zeros_like(k_pair)
        k_split = jnp.concatenate([jnp.where(low_half, k_pair, k_zero),
                                   jnp.where(low_half, k_zero, k_pair)], axis=0)
        s = lax.dot_general(q_pair, k_split, (((1,), (1,)), ((), ())),
                            preferred_element_type=F32) * dmat_ref[p]
        v_zero = jnp.zeros_like(v_pair)
        v_diag = jnp.concatenate([jnp.where(even_lanes, v_pair, v_zero),
                                  jnp.where(even_lanes, v_zero, v_pair)], axis=0)
        lhs = jnp.concatenate([s.astype(BF16), qc_scr[r, sub, cols]], axis=1)
        rhs = jnp.concatenate([v_diag, states[p].astype(BF16)], axis=0)
        o_pair = jnp.dot(lhs, rhs, preferred_element_type=F32)
        kv = lax.dot_general(kd_scr[r, sub, cols], v_pair, (((0,), (0,)), ((), ())),
                             preferred_element_type=F32)
        new_states.append(states[p] * cdec_ref[p] + jnp.where(own_block, kv, 0.0))
        for e in range(HEADS_PER_LANE_GROUP):
            h = p * HEADS_PER_LANE_GROUP + e
            hcols = slice(h * R_VAL_DIM, (h + 1) * R_VAL_DIM)
            o = o_pair[:, e * R_VAL_DIM:(e + 1) * R_VAL_DIM]
            mean = jnp.mean(o, axis=-1, keepdims=True)
            cen = o - mean
            var = jnp.mean(cen * cen, axis=-1, keepdims=True)
            normed = cen * lax.rsqrt(var + GN_EPS) * gain_ref[:, hcols]
            g = g_scr[r, sub, hcols].astype(F32)
            o_ref[out_rows, hcols] = (normed * _silu(g)).astype(BF16)
    return new_states


def _proj_kernel(steps_per_seq, x_ref, gain_ref, w_ref, cos_ref, sin_ref, cross_ref, sdec_ref,
                 dmat_ref, cdec_ref, gn_gain_ref, o_ref, ret_ref, state_ref,
                 q_scr, k_scr, qc_scr, kd_scr, v_scr, g_scr):
    @pl.when(pl.program_id(0) % steps_per_seq == 0)
    def _():
        state_ref[...] = jnp.zeros_like(state_ref)

    lane = lax.broadcasted_iota(jnp.int32, (1, COL_BLOCK), 1)
    first_half = (lane % R_KEY_DIM) < (R_KEY_DIM // 2)
    groups = COL_BLOCK // LANES
    n_pairs = R_HEADS // HEADS_PER_LANE_GROUP
    states = [state_ref[p] for p in range(n_pairs)]
    normed_rows = []
    for r in range(PROJ_ROWS // SEQ_BLOCK):
        rows = slice(r * SEQ_BLOCK, (r + 1) * SEQ_BLOCK)
        x = x_ref[rows, :]
        inv = lax.rsqrt(jnp.mean(x * x, axis=-1, keepdims=True) + NORM_EPS)
        h = (x * inv * gain_ref[...]).astype(BF16)

        for c in RETENTION_W_BLOCKS:
            acc = jnp.dot(h, w_ref[:, _block_cols(c)], preferred_element_type=F32)
            if c in (W_Q_R, W_K_R):
                partner = jnp.where(first_half,
                                    pltpu.roll(acc, COL_BLOCK - R_KEY_DIM // 2, 1),
                                    pltpu.roll(acc, R_KEY_DIM // 2, 1))
                cos = jnp.concatenate([cos_ref[rows, :]] * groups, axis=1)
                sin = jnp.concatenate([sin_ref[rows, :]] * groups, axis=1)
                acc = acc * cos + partner * sin
            if c == W_Q_R:
                acc = acc * (R_KEY_DIM ** -0.5)
                q_scr[r] = acc.astype(BF16)
                qc_scr[r] = (acc * cross_ref[...]).astype(BF16)
            elif c == W_K_R:
                k_scr[r] = acc.astype(BF16)
                kd_scr[r] = (acc * sdec_ref[...]).astype(BF16)
            elif c in W_V_R:
                v_scr[r, :, _block_cols(W_V_R.index(c))] = acc.astype(BF16)
            else:
                g_scr[r, :, _block_cols(W_G_R.index(c))] = acc.astype(BF16)
        for b in range(SEQ_BLOCK // RET_BLOCK):
            sub = slice(b * RET_BLOCK, (b + 1) * RET_BLOCK)
            out_rows = slice(r * SEQ_BLOCK + b * RET_BLOCK, r * SEQ_BLOCK + (b + 1) * RET_BLOCK)
            states = _retention_block(r, sub, out_rows, q_scr, k_scr, qc_scr, kd_scr, v_scr, g_scr,
                                      dmat_ref, cdec_ref, gn_gain_ref, ret_ref, states)

        normed_rows.append(h)

    h_all = jnp.concatenate(normed_rows, axis=0)
    for c, stored in STORED_BLOCK.items():
        acc = jnp.dot(h_all, w_ref[:, _block_cols(c)], preferred_element_type=F32)
        if c == W_Q_A:
            acc = acc * (A_HEAD_DIM ** -0.5 * LOG2_E)
        o_ref[:, _block_cols(stored)] = acc.astype(BF16)
    for p in range(n_pairs):
        state_ref[p] = states[p]


def _block_cols(c):
    return slice(c * COL_BLOCK, (c + 1) * COL_BLOCK)


def _projection_and_retention(x2d, gain, w_bf16, cos_t, sin_t, tables, gn_gain, seq_len):
    dmat, cross, sdec, cdec = tables
    n_tok = x2d.shape[0]
    steps_per_seq = seq_len // PROJ_ROWS
    row_groups = PROJ_ROWS // SEQ_BLOCK
    n_pairs = R_HEADS // HEADS_PER_LANE_GROUP
    narrow = pltpu.VMEM((row_groups, SEQ_BLOCK, R_QK_WIDTH), BF16)
    wide = pltpu.VMEM((row_groups, SEQ_BLOCK, R_V_WIDTH), BF16)
    return pl.pallas_call(
        functools.partial(_proj_kernel, steps_per_seq),
        grid=(n_tok // PROJ_ROWS,),
        in_specs=[
            pl.BlockSpec((PROJ_ROWS, D_MODEL), lambda i: (i, 0)),
            _resident((1, D_MODEL)),
            _resident((D_MODEL, IN_WIDTH)),
            pl.BlockSpec((PROJ_ROWS, LANES), lambda i: (i % steps_per_seq, 0)),
            pl.BlockSpec((PROJ_ROWS, LANES), lambda i: (i % steps_per_seq, 0)),
            _resident(cross.shape), _resident(sdec.shape), _resident(dmat.shape),
            _resident(cdec.shape), _resident(gn_gain.shape),
        ],
        out_specs=[pl.BlockSpec((PROJ_ROWS, PROJ_OUT_WIDTH), lambda i: (i, 0)),
                   pl.BlockSpec((PROJ_ROWS, R_V_WIDTH), lambda i: (i, 0))],
        out_shape=[jax.ShapeDtypeStruct((n_tok, PROJ_OUT_WIDTH), BF16),
                   jax.ShapeDtypeStruct((n_tok, R_V_WIDTH), BF16)],
        scratch_shapes=[pltpu.VMEM((n_pairs, LANES, HEADS_PER_LANE_GROUP * R_VAL_DIM), F32),
                        narrow, narrow, narrow, narrow, wide, wide],
        compiler_params=pltpu.CompilerParams(
            dimension_semantics=("arbitrary",), vmem_limit_bytes=VMEM_LIMIT_BYTES),
        name="projection_retention",
    )(x2d, gain, w_bf16, cos_t, sin_t, cross, sdec, dmat, cdec, gn_gain)


def _build_bias_tiles(ring_ref, bias_ref):
    n_keys = KEY_BLOCKS * SEQ_BLOCK
    sublane = lax.broadcasted_iota(jnp.int32, (SUBLANES, BIAS_RING), 0)
    lane = lax.broadcasted_iota(jnp.int32, (SUBLANES, SEQ_BLOCK), 1)
    masked = jnp.full((SUBLANES, SEQ_BLOCK), NEG_INF, F32)
    for h in range(A_HEADS):
        base = ring_ref[h]
        shift = 1
        while shift < SUBLANES:
            base = jnp.where((sublane & shift) != 0, pltpu.roll(base, shift, 1), base)
            shift *= 2
        for a in range(n_keys // SUBLANES):
            key_chunk = a * SUBLANES // CHUNK
            rows = slice(a * SUBLANES, (a + 1) * SUBLANES)
            tile = pltpu.roll(base, a * SUBLANES, 1)[:, :SEQ_BLOCK]
            in_band = ((lane >= (key_chunk - N_LEFT_CHUNKS) * CHUNK)
                       & (lane < (key_chunk + 1) * CHUNK))
            tile = jnp.where(in_band, tile, NEG_INF)
            for v in range(KEY_BLOCKS):
                in_sequence = a * SUBLANES >= (KEY_BLOCKS - 1 - v) * SEQ_BLOCK
                bias_ref[v, h, rows, :] = tile if in_sequence else masked


def _attention_block(sb, q_ref, k_prev_ref, k_cur_ref, v_prev_ref, v_cur_ref, g_ref, bias_ref,
                     s_refs):
    lane = lax.broadcasted_iota(jnp.int32, (1, LANES), 1)
    low_half = lane < A_HEAD_DIM
    halves = (low_half, ~low_half)
    row_half = (lax.broadcasted_iota(jnp.int32, (LANES, 1), 0) < A_HEAD_DIM)
    row_halves = (row_half, ~row_half)
    n_pairs = A_HEADS // HEADS_PER_LANE_GROUP
    rows = slice(sb * SEQ_BLOCK, (sb + 1) * SEQ_BLOCK)
    runtime_zero = jnp.minimum(pl.program_id(1), 0)
    variant = jnp.minimum(pl.program_id(1) * ATT_SUB + sb, KEY_BLOCKS - 1)

    def window(prev_ref, cur_ref, cols):
        blocks = []
        for back in range(KEY_BLOCKS - 1, -1, -1):
            blk = ATT_SUB + sb - back
            ref = prev_ref if blk < ATT_SUB else cur_ref
            start = (blk % ATT_SUB) * SEQ_BLOCK
            blocks.append(ref[start:start + SEQ_BLOCK, cols])
        return jnp.concatenate(blocks, axis=0)

    n_keys = KEY_BLOCKS * SEQ_BLOCK
    chunks_per_lane_group = LANES // CHUNK

    def band_rows(t):
        first_chunk = t * chunks_per_lane_group
        return slice(first_chunk * CHUNK,
                     (first_chunk + chunks_per_lane_group + N_LEFT_CHUNKS) * CHUNK)

    col_max = []
    for p in range(n_pairs):
        cols = slice(p * LANES, (p + 1) * LANES)
        q_pair = q_ref[rows, cols]
        k_pair = window(k_prev_ref, k_cur_ref, cols)
        for e in range(HEADS_PER_LANE_GROUP):
            h = p * HEADS_PER_LANE_GROUP + e
            q_h = jnp.where(halves[e], q_pair, jnp.zeros_like(q_pair))
            s = lax.dot_general(k_pair, q_h, (((1,), (1,)), ((), ())), preferred_element_type=F32)
            maxima = []
            for t in range(SEQ_BLOCK // LANES):
                band, lanes = band_rows(t), slice(t * LANES, (t + 1) * LANES)
                biased = s[band, lanes] + bias_ref[variant, h, band, lanes]
                s_refs[h][0, band, lanes] = biased
                maxima.append(jnp.max(biased, axis=0, keepdims=True))
            col_max.append(maxima)

    def second_pass():
        gated = []
        for p in range(n_pairs):
            cols = slice(p * LANES, (p + 1) * LANES)
            v_t = window(v_prev_ref, v_cur_ref, cols).T
            outs = []
            for e in range(HEADS_PER_LANE_GROUP):
                h = p * HEADS_PER_LANE_GROUP + e
                columns = []
                for t in range(SEQ_BLOCK // LANES):
                    band, lanes = band_rows(t), slice(t * LANES, (t + 1) * LANES)
                    in_band = jnp.exp2(s_refs[h][runtime_zero, band, lanes]
                                       - col_max[h][t]).astype(BF16)
                    pieces = [jnp.zeros((band.start, LANES), BF16)] if band.start else []
                    pieces.append(in_band)
                    if band.stop < n_keys:
                        pieces.append(jnp.zeros((n_keys - band.stop, LANES), BF16))
                    columns.append(jnp.concatenate(pieces, axis=0))
                probs = jnp.concatenate(columns, axis=1)
                v_h = jnp.where(row_halves[e], v_t, jnp.ones_like(v_t))
                acc = jnp.dot(v_h, probs, preferred_element_type=F32)
                own = slice(e * A_HEAD_DIM, (e + 1) * A_HEAD_DIM)
                other = slice((1 - e) * A_HEAD_DIM, (2 - e) * A_HEAD_DIM)
                outs.append(acc[own] / acc[other])
            o_pair = jnp.concatenate(outs, axis=0).T
            g = g_ref[rows, cols].astype(F32)
            gated.append((o_pair * _silu(g)).astype(BF16))
        return jnp.concatenate(gated, axis=1)

    return second_pass


def _attn_out_kernel(q_ref, k_prev_ref, k_cur_ref, v_prev_ref, v_cur_ref, g_ref, ring_ref,
                     ret_ref, ma_ref, mr_ref, x_ref, wa_ref, wr_ref, wo_ref, fg_ref, o_ref,
                     bias_ref, *s_refs):
    @pl.when((pl.program_id(0) == 0) & (pl.program_id(1) == 0))
    def _():
        _build_bias_tiles(ring_ref, bias_ref)

    second_passes = [
        _attention_block(sb, q_ref, k_prev_ref, k_cur_ref, v_prev_ref, v_cur_ref, g_ref, bias_ref,
                         s_refs[sb * A_HEADS:(sb + 1) * A_HEADS])
        for sb in range(ATT_SUB)]
    att = jnp.concatenate([second_pass() for second_pass in second_passes], axis=0)
    branch_a = jnp.dot(att, wa_ref[...], preferred_element_type=F32)
    branch_r = jnp.dot(ret_ref[...], wr_ref[...], preferred_element_type=F32)
    mixed = (_sigmoid(ma_ref[...].astype(F32)) * branch_a
             + _sigmoid(mr_ref[...].astype(F32)) * branch_r)
    y = x_ref[...] + jnp.dot(mixed.astype(BF16), wo_ref[...], preferred_element_type=F32)
    inv = lax.rsqrt(jnp.mean(y * y, axis=-1, keepdims=True) + NORM_EPS)
    o_ref[...] = y * inv * fg_ref[...]


def _attention_and_output(proj, ret, x2d, bias_ring, wa, wr, wo, final_gain, batch, seq_len):
    assert ATT_SUB >= KEY_BLOCKS - 1, "the key window must fit in the previous and current step"
    n_keys = KEY_BLOCKS * SEQ_BLOCK
    n_tok = proj.shape[0]
    step_rows = ATT_SUB * SEQ_BLOCK
    steps_per_seq = seq_len // step_rows

    def rows(b, j):
        return b * steps_per_seq + j

    def cur(col, width=COL_BLOCK):
        return pl.BlockSpec((step_rows, width), lambda b, j: (rows(b, j), col))

    def prev(col):
        return pl.BlockSpec((step_rows, COL_BLOCK),
                            lambda b, j: (rows(b, jnp.maximum(j - 1, 0)), col))

    return pl.pallas_call(
        _attn_out_kernel,
        grid=(batch, steps_per_seq),
        in_specs=[cur(COL_Q_A), prev(COL_K_A), cur(COL_K_A), prev(COL_V_A), cur(COL_V_A),
                  cur(COL_G_A), _resident(bias_ring.shape),
                  cur(0, R_V_WIDTH), cur(WIDE_M_A, D_MODEL), cur(WIDE_M_R, D_MODEL),
                  cur(0, D_MODEL),
                  _resident(wa.shape), _resident(wr.shape), _resident(wo.shape),
                  _resident(final_gain.shape)],
        out_specs=cur(0, D_MODEL),
        out_shape=jax.ShapeDtypeStruct((n_tok, D_MODEL), F32),
        scratch_shapes=[pltpu.VMEM((KEY_BLOCKS, A_HEADS, n_keys, SEQ_BLOCK), F32)]
        + [pltpu.VMEM((1, n_keys, SEQ_BLOCK), F32)] * (ATT_SUB * A_HEADS),
        compiler_params=pltpu.CompilerParams(
            dimension_semantics=("arbitrary", "arbitrary"), vmem_limit_bytes=VMEM_LIMIT_BYTES),
        name="attention_output",
    )(proj, proj, proj, proj, proj, proj, bias_ring, ret, proj, proj, x2d, wa, wr, wo, final_gain)


def _rotary_tables(seq_len):
    half = R_KEY_DIM // 2
    inv_freq = jnp.power(ROPE_BASE, -jnp.arange(half, dtype=F32) / half)
    ang = jnp.arange(seq_len).astype(F32)[:, None] * inv_freq[None, :]
    cos, sin = jnp.cos(ang), jnp.sin(ang)
    cos_t = jnp.tile(jnp.concatenate([cos, cos], axis=-1), (1, HEADS_PER_LANE_GROUP))
    sin_t = jnp.tile(jnp.concatenate([-sin, sin], axis=-1), (1, HEADS_PER_LANE_GROUP))
    return cos_t, sin_t


def _take_columns(table, idx):
    pieces, start = [], 0
    while start < len(idx):
        end = start + 1
        if end < len(idx) and idx[end] == idx[start]:
            while end < len(idx) and idx[end] == idx[start]:
                end += 1
            piece = jnp.broadcast_to(table[:, idx[start]:idx[start] + 1], (table.shape[0], end - start))
        else:
            while end < len(idx) and idx[end] == idx[end - 1] + 1:
                end += 1
            piece = table[:, idx[start]:idx[end - 1] + 1]
        pieces.append(piece)
        start = end
    return jnp.concatenate(pieces, axis=1)


def _bias_ring(rel_bias):
    n_keys = KEY_BLOCKS * SEQ_BLOCK
    assert n_keys + SEQ_BLOCK - 1 <= BIAS_RING
    u = np.arange(BIAS_RING)
    t = np.where(u < SEQ_BLOCK, u, u - BIAS_RING)
    idx = np.clip((KEY_BLOCKS - 1) * SEQ_BLOCK + t, -REL_CLIP, REL_CLIP) + REL_CLIP
    ring = _take_columns(rel_bias.astype(F32), idx) * LOG2_E
    return jnp.broadcast_to(ring[:, None, :], (rel_bias.shape[0], SUBLANES, BIAS_RING))


def _retention_tables():
    n_pairs = R_HEADS // HEADS_PER_LANE_GROUP
    log_gamma = jnp.log1p(-jnp.exp2(-5.0 - jnp.arange(R_HEADS, dtype=F32)))
    pos = jnp.arange(RET_BLOCK, dtype=F32)
    diff = pos[:, None] - pos[None, :]
    chunk_diff = (np.arange(RET_BLOCK)[:, None] // CHUNK) - (np.arange(RET_BLOCK)[None, :] // CHUNK)
    expo = jnp.where(chunk_diff == 0, jnp.abs(diff), diff)
    dmat = jnp.where((chunk_diff >= 0)[None], jnp.exp(expo[None] * log_gamma[:, None, None]), 0.0)
    dmat = dmat.reshape(n_pairs, HEADS_PER_LANE_GROUP, RET_BLOCK, RET_BLOCK)
    dmat = jnp.concatenate([dmat[:, e] for e in range(HEADS_PER_LANE_GROUP)], axis=-1)
    per_lane = lambda tab, width: jnp.tile(jnp.repeat(tab.T, width, axis=1), (SEQ_BLOCK // RET_BLOCK, 1))
    cross = per_lane(jnp.exp((pos + 1.0)[None, :] * log_gamma[:, None]), R_KEY_DIM)
    sdec = per_lane(jnp.exp((RET_BLOCK - 1.0 - pos)[None, :] * log_gamma[:, None]), R_KEY_DIM)
    block_decay = jnp.exp(RET_BLOCK * log_gamma)
    cdec = jnp.broadcast_to(jnp.repeat(block_decay, R_KEY_DIM).reshape(n_pairs, LANES, 1),
                            (n_pairs, LANES, HEADS_PER_LANE_GROUP * R_VAL_DIM))
    return dmat.astype(F32), cross, sdec, cdec


def kernel(x, norm_gain, w_in, rel_bias, gn_gain, w_out_attn, w_out_ret, w_out, final_gain):
    batch, seq_len, d_model = x.shape
    assert d_model == D_MODEL and norm_gain.shape[0] == 1, "single-layer block of width D_MODEL"
    assert seq_len % PROJ_ROWS == 0 and PROJ_ROWS % SEQ_BLOCK == 0
    assert seq_len % (ATT_SUB * SEQ_BLOCK) == 0
    x2d = x.reshape(batch * seq_len, D_MODEL)
    cos_t, sin_t = _rotary_tables(seq_len)
    proj, ret = _projection_and_retention(x2d, norm_gain[0][None, :], w_in[0].astype(BF16), cos_t, sin_t,
                                          _retention_tables(), gn_gain[0][None, :], seq_len)
    out = _attention_and_output(proj, ret, x2d, _bias_ring(rel_bias[0]), w_out_attn[0].astype(BF16),
                                w_out_ret[0].astype(BF16), w_out[0].astype(BF16),
                                final_gain[None, :], batch, seq_len)
    return out.reshape(batch, seq_len, D_MODEL)
```

```python
import functools

import numpy as np
import jax
import jax.numpy as jnp
from jax import lax
from jax.experimental import pallas as pl
from jax.experimental.pallas import tpu as pltpu

D_MODEL = 1024
CHUNK = 64
N_LEFT_CHUNKS = 8
A_HEADS = 8
A_HEAD_DIM = 64
A_WIDTH = A_HEADS * A_HEAD_DIM
REL_CLIP = 128
R_HEADS = 8
R_KEY_DIM = 64
R_VAL_DIM = 128
R_QK_WIDTH = R_HEADS * R_KEY_DIM
R_V_WIDTH = R_HEADS * R_VAL_DIM
ROPE_BASE = 10000.0
NORM_EPS = 1e-6
GN_EPS = 1e-5
NEG_INF = -1e30
LOG2_E = 1.4426950408889634
IN_WIDTH = 4 * A_WIDTH + 2 * R_QK_WIDTH + 2 * R_V_WIDTH + 2 * D_MODEL

COL_BLOCK = 512
N_COL_BLOCKS = IN_WIDTH // COL_BLOCK
W_Q_A, W_K_A, W_V_A, W_G_A, W_Q_R, W_K_R = 0, 1, 2, 3, 4, 5
W_V_R, W_G_R, W_M_A, W_M_R = (6, 7), (8, 9), (10, 11), (12, 13)
RETENTION_W_BLOCKS = (W_Q_R, W_K_R) + W_V_R + W_G_R
COL_Q_A, COL_K_A, COL_V_A, COL_G_A = 0, 1, 2, 3
STORED_BLOCK = {W_Q_A: COL_Q_A, W_K_A: COL_K_A, W_V_A: COL_V_A, W_G_A: COL_G_A,
                W_M_A[0]: 4, W_M_A[1]: 5}
PROJ_OUT_WIDTH = len(STORED_BLOCK) * COL_BLOCK
WIDE_M_A = 2

LANES = 128
SUBLANES = 8
HEADS_PER_LANE_GROUP = LANES // A_HEAD_DIM

PROJ_ROWS = 512
SEQ_BLOCK = 256
RET_BLOCK = 128
CHUNKS_PER_BLOCK = SEQ_BLOCK // CHUNK
ATT_SUB = 2
KEY_BLOCKS = N_LEFT_CHUNKS // CHUNKS_PER_BLOCK + 1
BIAS_RING = (KEY_BLOCKS + 1) * SEQ_BLOCK
VMEM_LIMIT_BYTES = 60 * 1024 * 1024

BF16 = jnp.bfloat16
F32 = jnp.float32


def _resident(shape):
    zeros = (0,) * len(shape)
    return pl.BlockSpec(shape, lambda *_: zeros, pipeline_mode=pl.Buffered(1))


def _sigmoid(x):
    return 0.5 * jnp.tanh(0.5 * x) + 0.5


def _silu(g):
    half = 0.5 * g
    return half * jnp.tanh(half) + half


def _retention_block(r, sub, out_rows, q_scr, k_scr, qc_scr, kd_scr, v_scr, g_scr, dmat_ref, cdec_ref,
                     gain_ref, o_ref, states):
    pair_lanes = HEADS_PER_LANE_GROUP * R_VAL_DIM
    low_half = lax.broadcasted_iota(jnp.int32, (1, LANES), 1) < R_KEY_DIM
    even_lanes = lax.broadcasted_iota(jnp.int32, (1, pair_lanes), 1) < R_VAL_DIM
    own_block = ((lax.broadcasted_iota(jnp.int32, (LANES, pair_lanes), 0) < R_KEY_DIM)
                 == (lax.broadcasted_iota(jnp.int32, (LANES, pair_lanes), 1) < R_VAL_DIM))
    new_states = []
    for p in range(R_HEADS // HEADS_PER_LANE_GROUP):
        cols = slice(p * LANES, (p + 1) * LANES)
        vcols = slice(p * pair_lanes, (p + 1) * pair_lanes)
        q_pair = q_scr[r, sub, cols]
        k_pair = k_scr[r, sub, cols]
        v_pair = v_scr[r, sub, vcols]
        k_zero = jnp.zeros_like(k_pair)
        k_split = jnp.concatenate([jnp.where(low_half, k_pair, k_zero),
                                   jnp.where(low_half, k_zero, k_pair)], axis=0)
        s = lax.dot_general(q_pair, k_split, (((1,), (1,)), ((), ())),
                            preferred_element_type=F32) * dmat_ref[p]
        v_zero = jnp.zeros_like(v_pair)
        v_diag = jnp.concatenate([jnp.where(even_lanes, v_pair, v_zero),
                                  jnp.where(even_lanes, v_zero, v_pair)], axis=0)
        lhs = jnp.concatenate([s.astype(BF16), qc_scr[r, sub, cols]], axis=1)
        rhs = jnp.concatenate([v_diag, states[p].astype(BF16)], axis=0)
        o_pair = jnp.dot(lhs, rhs, preferred_element_type=F32)
        kv = lax.dot_general(kd_scr[r, sub, cols], v_pair, (((0,), (0,)), ((), ())),
                             preferred_element_type=F32)
        new_states.append(states[p] * cdec_ref[p] + jnp.where(own_block, kv, 0.0))
        for e in range(HEADS_PER_LANE_GROUP):
            h = p * HEADS_PER_LANE_GROUP + e
            hcols = slice(h * R_VAL_DIM, (h + 1) * R_VAL_DIM)
            o = o_pair[:, e * R_VAL_DIM:(e + 1) * R_VAL_DIM]
            mean = jnp.mean(o, axis=-1, keepdims=True)
            cen = o - mean
            var = jnp.mean(cen * cen, axis=-1, keepdims=True)
            normed = cen * lax.rsqrt(var + GN_EPS) * gain_ref[:, hcols]
            g = g_scr[r, sub, hcols].astype(F32)
            o_ref[out_rows, hcols] = (normed * _silu(g)).astype(BF16)
    return new_states


def _proj_kernel(steps_per_seq, x_ref, gain_ref, w_ref, cos_ref, sin_ref, cross_ref, sdec_ref,
                 dmat_ref, cdec_ref, gn_gain_ref, wr_ref, o_ref, gated_ref, state_ref,
                 q_scr, k_scr, qc_scr, kd_scr, v_scr, g_scr, ret_scr):
    @pl.when(pl.program_id(0) % steps_per_seq == 0)
    def _():
        state_ref[...] = jnp.zeros_like(state_ref)

    lane = lax.broadcasted_iota(jnp.int32, (1, COL_BLOCK), 1)
    first_half = (lane % R_KEY_DIM) < (R_KEY_DIM // 2)
    groups = COL_BLOCK // LANES
    n_pairs = R_HEADS // HEADS_PER_LANE_GROUP
    states = [state_ref[p] for p in range(n_pairs)]
    for r in range(PROJ_ROWS // SEQ_BLOCK):
        rows = slice(r * SEQ_BLOCK, (r + 1) * SEQ_BLOCK)
        x = x_ref[rows, :]
        inv = lax.rsqrt(jnp.mean(x * x, axis=-1, keepdims=True) + NORM_EPS)
        h = (x * inv * gain_ref[...]).astype(BF16)

        for c in RETENTION_W_BLOCKS:
            acc = jnp.dot(h, w_ref[:, _block_cols(c)], preferred_element_type=F32)
            if c in (W_Q_R, W_K_R):
                partner = jnp.where(first_half,
                                    pltpu.roll(acc, COL_BLOCK - R_KEY_DIM // 2, 1),
                                    pltpu.roll(acc, R_KEY_DIM // 2, 1))
                cos = jnp.concatenate([cos_ref[rows, :]] * groups, axis=1)
                sin = jnp.concatenate([sin_ref[rows, :]] * groups, axis=1)
                acc = acc * cos + partner * sin
            if c == W_Q_R:
                acc = acc * (R_KEY_DIM ** -0.5)
                q_scr[r] = acc.astype(BF16)
                qc_scr[r] = (acc * cross_ref[...]).astype(BF16)
            elif c == W_K_R:
                k_scr[r] = acc.astype(BF16)
                kd_scr[r] = (acc * sdec_ref[...]).astype(BF16)
            elif c in W_V_R:
                v_scr[r, :, _block_cols(W_V_R.index(c))] = acc.astype(BF16)
            else:
                g_scr[r, :, _block_cols(W_G_R.index(c))] = acc.astype(BF16)
        for b in range(SEQ_BLOCK // RET_BLOCK):
            sub = slice(b * RET_BLOCK, (b + 1) * RET_BLOCK)
            out_rows = slice(r * SEQ_BLOCK + b * RET_BLOCK, r * SEQ_BLOCK + (b + 1) * RET_BLOCK)
            states = _retention_block(r, sub, out_rows, q_scr, k_scr, qc_scr, kd_scr, v_scr, g_scr,
                                      dmat_ref, cdec_ref, gn_gain_ref, ret_scr, states)

        for c, stored in STORED_BLOCK.items():
            acc = jnp.dot(h, w_ref[:, _block_cols(c)], preferred_element_type=F32)
            if c == W_Q_A:
                acc = acc * (A_HEAD_DIM ** -0.5 * LOG2_E)
            o_ref[rows, _block_cols(stored)] = acc.astype(BF16)

        m_r = jnp.dot(h, w_ref[:, W_M_R[0] * COL_BLOCK:(W_M_R[-1] + 1) * COL_BLOCK],
                      preferred_element_type=F32)
        branch_r = jnp.dot(ret_scr[rows, :], wr_ref[...], preferred_element_type=F32)
        gated_ref[rows, :] = (_sigmoid(m_r) * branch_r).astype(BF16)
    for p in range(n_pairs):
        state_ref[p] = states[p]


def _block_cols(c):
    return slice(c * COL_BLOCK, (c + 1) * COL_BLOCK)


def _projection_and_retention(x2d, gain, w_bf16, cos_t, sin_t, tables, gn_gain, wr, seq_len):
    dmat, cross, sdec, cdec = tables
    n_tok = x2d.shape[0]
    steps_per_seq = seq_len // PROJ_ROWS
    row_groups = PROJ_ROWS // SEQ_BLOCK
    n_pairs = R_HEADS // HEADS_PER_LANE_GROUP
    narrow = pltpu.VMEM((row_groups, SEQ_BLOCK, R_QK_WIDTH), BF16)
    wide = pltpu.VMEM((row_groups, SEQ_BLOCK, R_V_WIDTH), BF16)
    return pl.pallas_call(
        functools.partial(_proj_kernel, steps_per_seq),
        grid=(n_tok // PROJ_ROWS,),
        in_specs=[
            pl.BlockSpec((PROJ_ROWS, D_MODEL), lambda i: (i, 0)),
            _resident((1, D_MODEL)),
            _resident((D_MODEL, IN_WIDTH)),
            pl.BlockSpec((PROJ_ROWS, LANES), lambda i: (i % steps_per_seq, 0)),
            pl.BlockSpec((PROJ_ROWS, LANES), lambda i: (i % steps_per_seq, 0)),
            _resident(cross.shape), _resident(sdec.shape), _resident(dmat.shape),
            _resident(cdec.shape), _resident(gn_gain.shape), _resident(wr.shape),
        ],
        out_specs=[pl.BlockSpec((PROJ_ROWS, PROJ_OUT_WIDTH), lambda i: (i, 0)),
                   pl.BlockSpec((PROJ_ROWS, D_MODEL), lambda i: (i, 0))],
        out_shape=[jax.ShapeDtypeStruct((n_tok, PROJ_OUT_WIDTH), BF16),
                   jax.ShapeDtypeStruct((n_tok, D_MODEL), BF16)],
        scratch_shapes=[pltpu.VMEM((n_pairs, LANES, HEADS_PER_LANE_GROUP * R_VAL_DIM), F32),
                        narrow, narrow, narrow, narrow, wide, wide,
                        pltpu.VMEM((PROJ_ROWS, R_V_WIDTH), BF16)],
        compiler_params=pltpu.CompilerParams(
            dimension_semantics=("arbitrary",), vmem_limit_bytes=VMEM_LIMIT_BYTES),
        name="projection_retention",
    )(x2d, gain, w_bf16, cos_t, sin_t, cross, sdec, dmat, cdec, gn_gain, wr)


def _build_bias_tiles(ring_ref, bias_ref):
    n_keys = KEY_BLOCKS * SEQ_BLOCK
    sublane = lax.broadcasted_iota(jnp.int32, (SUBLANES, BIAS_RING), 0)
    lane = lax.broadcasted_iota(jnp.int32, (SUBLANES, SEQ_BLOCK), 1)
    masked = jnp.full((SUBLANES, SEQ_BLOCK), NEG_INF, F32)
    for h in range(A_HEADS):
        base = ring_ref[h]
        shift = 1
        while shift < SUBLANES:
            base = jnp.where((sublane & shift) != 0, pltpu.roll(base, shift, 1), base)
            shift *= 2
        for a in range(n_keys // SUBLANES):
            key_chunk = a * SUBLANES // CHUNK
            rows = slice(a * SUBLANES, (a + 1) * SUBLANES)
            tile = pltpu.roll(base, a * SUBLANES, 1)[:, :SEQ_BLOCK]
            in_band = ((lane >= (key_chunk - N_LEFT_CHUNKS) * CHUNK)
                       & (lane < (key_chunk + 1) * CHUNK))
            tile = jnp.where(in_band, tile, NEG_INF)
            for v in range(KEY_BLOCKS):
                in_sequence = a * SUBLANES >= (KEY_BLOCKS - 1 - v) * SEQ_BLOCK
                bias_ref[v, h, rows, :] = tile if in_sequence else masked


def _attention_block(sb, q_ref, k_prev_ref, k_cur_ref, v_prev_ref, v_cur_ref, g_ref, bias_ref,
                     s_refs):
    lane = lax.broadcasted_iota(jnp.int32, (1, LANES), 1)
    low_half = lane < A_HEAD_DIM
    halves = (low_half, ~low_half)
    row_half = (lax.broadcasted_iota(jnp.int32, (LANES, 1), 0) < A_HEAD_DIM)
    row_halves = (row_half, ~row_half)
    n_pairs = A_HEADS // HEADS_PER_LANE_GROUP
    rows = slice(sb * SEQ_BLOCK, (sb + 1) * SEQ_BLOCK)
    runtime_zero = jnp.minimum(pl.program_id(1), 0)
    variant = jnp.minimum(pl.program_id(1) * ATT_SUB + sb, KEY_BLOCKS - 1)

    def window(prev_ref, cur_ref, cols):
        blocks = []
        for back in range(KEY_BLOCKS - 1, -1, -1):
            blk = ATT_SUB + sb - back
            ref = prev_ref if blk < ATT_SUB else cur_ref
            start = (blk % ATT_SUB) * SEQ_BLOCK
            blocks.append(ref[start:start + SEQ_BLOCK, cols])
        return jnp.concatenate(blocks, axis=0)

    n_keys = KEY_BLOCKS * SEQ_BLOCK
    chunks_per_lane_group = LANES // CHUNK

    def band_rows(t):
        first_chunk = t * chunks_per_lane_group
        return slice(first_chunk * CHUNK,
                     (first_chunk + chunks_per_lane_group + N_LEFT_CHUNKS) * CHUNK)

    col_max = []
    for p in range(n_pairs):
        cols = slice(p * LANES, (p + 1) * LANES)
        q_pair = q_ref[rows, cols]
        k_pair = window(k_prev_ref, k_cur_ref, cols)
        for e in range(HEADS_PER_LANE_GROUP):
            h = p * HEADS_PER_LANE_GROUP + e
            q_h = jnp.where(halves[e], q_pair, jnp.zeros_like(q_pair))
            s = lax.dot_general(k_pair, q_h, (((1,), (1,)), ((), ())), preferred_element_type=F32)
            maxima = []
            for t in range(SEQ_BLOCK // LANES):
                band, lanes = band_rows(t), slice(t * LANES, (t + 1) * LANES)
                biased = s[band, lanes] + bias_ref[variant, h, band, lanes]
                s_refs[h][0, band, lanes] = biased
                maxima.append(jnp.max(biased, axis=0, keepdims=True))
            col_max.append(maxima)

    def second_pass():
        gated = []
        for p in range(n_pairs):
            cols = slice(p * LANES, (p + 1) * LANES)
            v_t = window(v_prev_ref, v_cur_ref, cols).T
            outs = []
            for e in range(HEADS_PER_LANE_GROUP):
                h = p * HEADS_PER_LANE_GROUP + e
                columns = []
                for t in range(SEQ_BLOCK // LANES):
                    band, lanes = band_rows(t), slice(t * LANES, (t + 1) * LANES)
                    in_band = jnp.exp2(s_refs[h][runtime_zero, band, lanes]
                                       - col_max[h][t]).astype(BF16)
                    pieces = [jnp.zeros((band.start, LANES), BF16)] if band.start else []
                    pieces.append(in_band)
                    if band.stop < n_keys:
                        pieces.append(jnp.zeros((n_keys - band.stop, LANES), BF16))
                    columns.append(jnp.concatenate(pieces, axis=0))
                probs = jnp.concatenate(columns, axis=1)
                v_h = jnp.where(row_halves[e], v_t, jnp.ones_like(v_t))
                acc = jnp.dot(v_h, probs, preferred_element_type=F32)
                own = slice(e * A_HEAD_DIM, (e + 1) * A_HEAD_DIM)
                other = slice((1 - e) * A_HEAD_DIM, (2 - e) * A_HEAD_DIM)
                outs.append(acc[own] / acc[other])
            o_pair = jnp.concatenate(outs, axis=0).T
            g = g_ref[rows, cols].astype(F32)
            gated.append((o_pair * _silu(g)).astype(BF16))
        return jnp.concatenate(gated, axis=1)

    return second_pass


def _attn_out_kernel(q_ref, k_prev_ref, k_cur_ref, v_prev_ref, v_cur_ref, g_ref, ring_ref,
                     gated_r_ref, ma_ref, x_ref, wa_ref, wo_ref, fg_ref, o_ref,
                     bias_ref, *s_refs):
    @pl.when((pl.program_id(0) == 0) & (pl.program_id(1) == 0))
    def _():
        _build_bias_tiles(ring_ref, bias_ref)

    second_passes = [
        _attention_block(sb, q_ref, k_prev_ref, k_cur_ref, v_prev_ref, v_cur_ref, g_ref, bias_ref,
                         s_refs[sb * A_HEADS:(sb + 1) * A_HEADS])
        for sb in range(ATT_SUB)]
    att = jnp.concatenate([second_pass() for second_pass in second_passes], axis=0)
    branch_a = jnp.dot(att, wa_ref[...], preferred_element_type=F32)
    mixed = _sigmoid(ma_ref[...].astype(F32)) * branch_a + gated_r_ref[...].astype(F32)
    y = x_ref[...] + jnp.dot(mixed.astype(BF16), wo_ref[...], preferred_element_type=F32)
    inv = lax.rsqrt(jnp.mean(y * y, axis=-1, keepdims=True) + NORM_EPS)
    o_ref[...] = y * inv * fg_ref[...]


def _attention_and_output(proj, gated_r, x2d, bias_ring, wa, wo, final_gain, batch, seq_len):
    assert ATT_SUB >= KEY_BLOCKS - 1, "the key window must fit in the previous and current step"
    n_keys = KEY_BLOCKS * SEQ_BLOCK
    n_tok = proj.shape[0]
    step_rows = ATT_SUB * SEQ_BLOCK
    steps_per_seq = seq_len // step_rows

    def rows(b, j):
        return b * steps_per_seq + j

    def cur(col, width=COL_BLOCK):
        return pl.BlockSpec((step_rows, width), lambda b, j: (rows(b, j), col))

    def prev(col):
        return pl.BlockSpec((step_rows, COL_BLOCK),
                            lambda b, j: (rows(b, jnp.maximum(j - 1, 0)), col))

    return pl.pallas_call(
        _attn_out_kernel,
        grid=(batch, steps_per_seq),
        in_specs=[cur(COL_Q_A), prev(COL_K_A), cur(COL_K_A), prev(COL_V_A), cur(COL_V_A),
                  cur(COL_G_A), _resident(bias_ring.shape),
                  cur(0, D_MODEL), cur(WIDE_M_A, D_MODEL),
                  cur(0, D_MODEL),
                  _resident(wa.shape), _resident(wo.shape),
                  _resident(final_gain.shape)],
        out_specs=cur(0, D_MODEL),
        out_shape=jax.ShapeDtypeStruct((n_tok, D_MODEL), F32),
        scratch_shapes=[pltpu.VMEM((KEY_BLOCKS, A_HEADS, n_keys, SEQ_BLOCK), F32)]
        + [pltpu.VMEM((1, n_keys, SEQ_BLOCK), F32)] * (ATT_SUB * A_HEADS),
        compiler_params=pltpu.CompilerParams(
            dimension_semantics=("arbitrary", "arbitrary"), vmem_limit_bytes=VMEM_LIMIT_BYTES),
        name="attention_output",
    )(proj, proj, proj, proj, proj, proj, bias_ring, gated_r, proj, x2d, wa, wo, final_gain)


def _rotary_tables(seq_len):
    half = R_KEY_DIM // 2
    inv_freq = jnp.power(ROPE_BASE, -jnp.arange(half, dtype=F32) / half)
    ang = jnp.arange(seq_len).astype(F32)[:, None] * inv_freq[None, :]
    cos, sin = jnp.cos(ang), jnp.sin(ang)
    cos_t = jnp.tile(jnp.concatenate([cos, cos], axis=-1), (1, HEADS_PER_LANE_GROUP))
    sin_t = jnp.tile(jnp.concatenate([-sin, sin], axis=-1), (1, HEADS_PER_LANE_GROUP))
    return cos_t, sin_t


def _take_columns(table, idx):
    pieces, start = [], 0
    while start < len(idx):
        end = start + 1
        if end < len(idx) and idx[end] == idx[start]:
            while end < len(idx) and idx[end] == idx[start]:
                end += 1
            piece = jnp.broadcast_to(table[:, idx[start]:idx[start] + 1], (table.shape[0], end - start))
        else:
            while end < len(idx) and idx[end] == idx[end - 1] + 1:
                end += 1
            piece = table[:, idx[start]:idx[end - 1] + 1]
        pieces.append(piece)
        start = end
    return jnp.concatenate(pieces, axis=1)


def _bias_ring(rel_bias):
    n_keys = KEY_BLOCKS * SEQ_BLOCK
    assert n_keys + SEQ_BLOCK - 1 <= BIAS_RING
    u = np.arange(BIAS_RING)
    t = np.where(u < SEQ_BLOCK, u, u - BIAS_RING)
    idx = np.clip((KEY_BLOCKS - 1) * SEQ_BLOCK + t, -REL_CLIP, REL_CLIP) + REL_CLIP
    ring = _take_columns(rel_bias.astype(F32), idx) * LOG2_E
    return jnp.broadcast_to(ring[:, None, :], (rel_bias.shape[0], SUBLANES, BIAS_RING))


def _retention_tables():
    n_pairs = R_HEADS // HEADS_PER_LANE_GROUP
    log_gamma = jnp.log1p(-jnp.exp2(-5.0 - jnp.arange(R_HEADS, dtype=F32)))
    pos = jnp.arange(RET_BLOCK, dtype=F32)
    diff = pos[:, None] - pos[None, :]
    chunk_diff = (np.arange(RET_BLOCK)[:, None] // CHUNK) - (np.arange(RET_BLOCK)[None, :] // CHUNK)
    expo = jnp.where(chunk_diff == 0, jnp.abs(diff), diff)
    dmat = jnp.where((chunk_diff >= 0)[None], jnp.exp(expo[None] * log_gamma[:, None, None]), 0.0)
    dmat = dmat.reshape(n_pairs, HEADS_PER_LANE_GROUP, RET_BLOCK, RET_BLOCK)
    dmat = jnp.concatenate([dmat[:, e] for e in range(HEADS_PER_LANE_GROUP)], axis=-1)
    per_lane = lambda tab, width: jnp.tile(jnp.repeat(tab.T, width, axis=1), (SEQ_BLOCK // RET_BLOCK, 1))
    cross = per_lane(jnp.exp((pos + 1.0)[None, :] * log_gamma[:, None]), R_KEY_DIM)
    sdec = per_lane(jnp.exp((RET_BLOCK - 1.0 - pos)[None, :] * log_gamma[:, None]), R_KEY_DIM)
    block_decay = jnp.exp(RET_BLOCK * log_gamma)
    cdec = jnp.broadcast_to(jnp.repeat(block_decay, R_KEY_DIM).reshape(n_pairs, LANES, 1),
                            (n_pairs, LANES, HEADS_PER_LANE_GROUP * R_VAL_DIM))
    return dmat.astype(F32), cross, sdec, cdec


def kernel(x, norm_gain, w_in, rel_bias, gn_gain, w_out_attn, w_out_ret, w_out, final_gain):
    batch, seq_len, d_model = x.shape
    assert d_model == D_MODEL and norm_gain.shape[0] == 1, "single-layer block of width D_MODEL"
    assert seq_len % PROJ_ROWS == 0 and PROJ_ROWS % SEQ_BLOCK == 0
    assert seq_len % (ATT_SUB * SEQ_BLOCK) == 0
    x2d = x.reshape(batch * seq_len, D_MODEL)
    cos_t, sin_t = _rotary_tables(seq_len)
    proj, gated_r = _projection_and_retention(
        x2d, norm_gain[0][None, :], w_in[0].astype(BF16), cos_t, sin_t, _retention_tables(),
        gn_gain[0][None, :], w_out_ret[0].astype(BF16), seq_len)
    out = _attention_and_output(proj, gated_r, x2d, _bias_ring(rel_bias[0]),
                                w_out_attn[0].astype(BF16), w_out[0].astype(BF16),
                                final_gain[None, :], batch, seq_len)
    return out.reshape(batch, seq_len, D_MODEL)
```

```python
import functools

import numpy as np
import jax
import jax.numpy as jnp
from jax import lax
from jax.experimental import pallas as pl
from jax.experimental.pallas import tpu as pltpu

D_MODEL = 1024
CHUNK = 64
N_LEFT_CHUNKS = 8
A_HEADS = 8
A_HEAD_DIM = 64
A_WIDTH = A_HEADS * A_HEAD_DIM
REL_CLIP = 128
R_HEADS = 8
R_KEY_DIM = 64
R_VAL_DIM = 128
R_QK_WIDTH = R_HEADS * R_KEY_DIM
R_V_WIDTH = R_HEADS * R_VAL_DIM
ROPE_BASE = 10000.0
NORM_EPS = 1e-6
GN_EPS = 1e-5
NEG_INF = -1e30
LOG2_E = 1.4426950408889634
IN_WIDTH = 4 * A_WIDTH + 2 * R_QK_WIDTH + 2 * R_V_WIDTH + 2 * D_MODEL

COL_BLOCK = 512
N_COL_BLOCKS = IN_WIDTH // COL_BLOCK
W_Q_A, W_K_A, W_V_A, W_G_A, W_Q_R, W_K_R = 0, 1, 2, 3, 4, 5
W_V_R, W_G_R, W_M_A, W_M_R = (6, 7), (8, 9), (10, 11), (12, 13)
RETENTION_W_BLOCKS = (W_Q_R, W_K_R) + W_V_R + W_G_R
COL_Q_A, COL_K_A, COL_V_A, COL_G_A = 0, 1, 2, 3
STORED_BLOCK = {W_Q_A: COL_Q_A, W_K_A: COL_K_A, W_V_A: COL_V_A, W_G_A: COL_G_A,
                W_M_A[0]: 4, W_M_A[1]: 5, W_M_R[0]: 6, W_M_R[1]: 7}
PROJ_OUT_WIDTH = len(STORED_BLOCK) * COL_BLOCK
WIDE_M_A, WIDE_M_R = 2, 3

LANES = 128
SUBLANES = 8
DENOM_ROWS = 16
HEADS_PER_LANE_GROUP = LANES // A_HEAD_DIM

PROJ_ROWS = 512
SEQ_BLOCK = 256
RET_BLOCK = 128
CHUNKS_PER_BLOCK = SEQ_BLOCK // CHUNK
ATT_SUB = 2
KEY_BLOCKS = N_LEFT_CHUNKS // CHUNKS_PER_BLOCK + 1
BIAS_RING = (KEY_BLOCKS + 1) * SEQ_BLOCK
VMEM_LIMIT_BYTES = 60 * 1024 * 1024

BF16 = jnp.bfloat16
F32 = jnp.float32


def _resident(shape):
    zeros = (0,) * len(shape)
    return pl.BlockSpec(shape, lambda *_: zeros, pipeline_mode=pl.Buffered(1))


def _sigmoid(x):
    return 0.5 * jnp.tanh(0.5 * x) + 0.5


def _silu(g):
    half = 0.5 * g
    return half * jnp.tanh(half) + half


def _retention_block(r, sub, out_rows, q_scr, k_scr, qc_scr, kd_scr, v_scr, g_scr, dmat_ref, cdec_ref,
                     gain_ref, o_ref, states):
    pair_lanes = HEADS_PER_LANE_GROUP * R_VAL_DIM
    low_half = lax.broadcasted_iota(jnp.int32, (1, LANES), 1) < R_KEY_DIM
    even_lanes = lax.broadcasted_iota(jnp.int32, (1, pair_lanes), 1) < R_VAL_DIM
    own_block = ((lax.broadcasted_iota(jnp.int32, (LANES, pair_lanes), 0) < R_KEY_DIM)
                 == (lax.broadcasted_iota(jnp.int32, (LANES, pair_lanes), 1) < R_VAL_DIM))
    new_states = []
    for p in range(R_HEADS // HEADS_PER_LANE_GROUP):
        cols = slice(p * LANES, (p + 1) * LANES)
        vcols = slice(p * pair_lanes, (p + 1) * pair_lanes)
        q_pair = q_scr[r, sub, cols]
        k_pair = k_scr[r, sub, cols]
        v_pair = v_scr[r, sub, vcols]
        k_zero = jnp.zeros_like(k_pair)
        k_split = jnp.concatenate([jnp.where(low_half, k_pair, k_zero),
                                   jnp.where(low_half, k_zero, k_pair)], axis=0)
        s = lax.dot_general(q_pair, k_split, (((1,), (1,)), ((), ())),
                            preferred_element_type=F32) * dmat_ref[p]
        v_zero = jnp.zeros_like(v_pair)
        v_diag = jnp.concatenate([jnp.where(even_lanes, v_pair, v_zero),
                                  jnp.where(even_lanes, v_zero, v_pair)], axis=0)
        lhs = jnp.concatenate([s.astype(BF16), qc_scr[r, sub, cols]], axis=1)
        rhs = jnp.concatenate([v_diag, states[p].astype(BF16)], axis=0)
        o_pair = jnp.dot(lhs, rhs, preferred_element_type=F32)
        kv = lax.dot_general(kd_scr[r, sub, cols], v_pair, (((0,), (0,)), ((), ())),
                             preferred_element_type=F32)
        new_states.append(states[p] * cdec_ref[p] + jnp.where(own_block, kv, 0.0))
        for e in range(HEADS_PER_LANE_GROUP):
            h = p * HEADS_PER_LANE_GROUP + e
            hcols = slice(h * R_VAL_DIM, (h + 1) * R_VAL_DIM)
            o = o_pair[:, e * R_VAL_DIM:(e + 1) * R_VAL_DIM]
            mean = jnp.mean(o, axis=-1, keepdims=True)
            cen = o - mean
            var = jnp.mean(cen * cen, axis=-1, keepdims=True)
            normed = cen * lax.rsqrt(var + GN_EPS) * gain_ref[:, hcols]
            g = g_scr[r, sub, hcols].astype(F32)
            o_ref[out_rows, hcols] = (normed * _silu(g)).astype(BF16)
    return new_states


def _proj_kernel(steps_per_seq, x_ref, gain_ref, w_ref, cos_ref, sin_ref, cross_ref, sdec_ref,
                 dmat_ref, cdec_ref, gn_gain_ref, o_ref, ret_ref, state_ref,
                 q_scr, k_scr, qc_scr, kd_scr, v_scr, g_scr):
    @pl.when(pl.program_id(0) % steps_per_seq == 0)
    def _():
        state_ref[...] = jnp.zeros_like(state_ref)

    lane = lax.broadcasted_iota(jnp.int32, (1, COL_BLOCK), 1)
    first_half = (lane % R_KEY_DIM) < (R_KEY_DIM // 2)
    groups = COL_BLOCK // LANES
    n_pairs = R_HEADS // HEADS_PER_LANE_GROUP
    states = [state_ref[p] for p in range(n_pairs)]
    for r in range(PROJ_ROWS // SEQ_BLOCK):
        rows = slice(r * SEQ_BLOCK, (r + 1) * SEQ_BLOCK)
        x = x_ref[rows, :]
        inv = lax.rsqrt(jnp.mean(x * x, axis=-1, keepdims=True) + NORM_EPS)
        h = (x * inv * gain_ref[...]).astype(BF16)

        for c in RETENTION_W_BLOCKS:
            acc = jnp.dot(h, w_ref[:, _block_cols(c)], preferred_element_type=F32)
            if c in (W_Q_R, W_K_R):
                partner = jnp.where(first_half,
                                    pltpu.roll(acc, COL_BLOCK - R_KEY_DIM // 2, 1),
                                    pltpu.roll(acc, R_KEY_DIM // 2, 1))
                cos = jnp.concatenate([cos_ref[rows, :]] * groups, axis=1)
                sin = jnp.concatenate([sin_ref[rows, :]] * groups, axis=1)
                acc = acc * cos + partner * sin
            if c == W_Q_R:
                acc = acc * (R_KEY_DIM ** -0.5)
                q_scr[r] = acc.astype(BF16)
                qc_scr[r] = (acc * cross_ref[...]).astype(BF16)
            elif c == W_K_R:
                k_scr[r] = acc.astype(BF16)
                kd_scr[r] = (acc * sdec_ref[...]).astype(BF16)
            elif c in W_V_R:
                v_scr[r, :, _block_cols(W_V_R.index(c))] = acc.astype(BF16)
            else:
                g_scr[r, :, _block_cols(W_G_R.index(c))] = acc.astype(BF16)
        for b in range(SEQ_BLOCK // RET_BLOCK):
            sub = slice(b * RET_BLOCK, (b + 1) * RET_BLOCK)
            out_rows = slice(r * SEQ_BLOCK + b * RET_BLOCK, r * SEQ_BLOCK + (b + 1) * RET_BLOCK)
            states = _retention_block(r, sub, out_rows, q_scr, k_scr, qc_scr, kd_scr, v_scr, g_scr,
                                      dmat_ref, cdec_ref, gn_gain_ref, ret_ref, states)

        for c, stored in STORED_BLOCK.items():
            acc = jnp.dot(h, w_ref[:, _block_cols(c)], preferred_element_type=F32)
            if c == W_Q_A:
                acc = acc * (A_HEAD_DIM ** -0.5 * LOG2_E)
            o_ref[rows, _block_cols(stored)] = acc.astype(BF16)
    for p in range(n_pairs):
        state_ref[p] = states[p]


def _block_cols(c):
    return slice(c * COL_BLOCK, (c + 1) * COL_BLOCK)


def _projection_and_retention(x2d, gain, w_bf16, cos_t, sin_t, tables, gn_gain, seq_len):
    dmat, cross, sdec, cdec = tables
    n_tok = x2d.shape[0]
    steps_per_seq = seq_len // PROJ_ROWS
    row_groups = PROJ_ROWS // SEQ_BLOCK
    n_pairs = R_HEADS // HEADS_PER_LANE_GROUP
    narrow = pltpu.VMEM((row_groups, SEQ_BLOCK, R_QK_WIDTH), BF16)
    wide = pltpu.VMEM((row_groups, SEQ_BLOCK, R_V_WIDTH), BF16)
    return pl.pallas_call(
        functools.partial(_proj_kernel, steps_per_seq),
        grid=(n_tok // PROJ_ROWS,),
        in_specs=[
            pl.BlockSpec((PROJ_ROWS, D_MODEL), lambda i: (i, 0)),
            _resident((1, D_MODEL)),
            _resident((D_MODEL, IN_WIDTH)),
            pl.BlockSpec((PROJ_ROWS, LANES), lambda i: (i % steps_per_seq, 0)),
            pl.BlockSpec((PROJ_ROWS, LANES), lambda i: (i % steps_per_seq, 0)),
            _resident(cross.shape), _resident(sdec.shape), _resident(dmat.shape),
            _resident(cdec.shape), _resident(gn_gain.shape),
        ],
        out_specs=[pl.BlockSpec((PROJ_ROWS, PROJ_OUT_WIDTH), lambda i: (i, 0)),
                   pl.BlockSpec((PROJ_ROWS, R_V_WIDTH), lambda i: (i, 0))],
        out_shape=[jax.ShapeDtypeStruct((n_tok, PROJ_OUT_WIDTH), BF16),
                   jax.ShapeDtypeStruct((n_tok, R_V_WIDTH), BF16)],
        scratch_shapes=[pltpu.VMEM((n_pairs, LANES, HEADS_PER_LANE_GROUP * R_VAL_DIM), F32),
                        narrow, narrow, narrow, narrow, wide, wide],
        compiler_params=pltpu.CompilerParams(
            dimension_semantics=("arbitrary",), vmem_limit_bytes=VMEM_LIMIT_BYTES),
        name="projection_retention",
    )(x2d, gain, w_bf16, cos_t, sin_t, cross, sdec, dmat, cdec, gn_gain)


def _build_bias_tiles(ring_ref, bias_ref):
    n_keys = KEY_BLOCKS * SEQ_BLOCK
    sublane = lax.broadcasted_iota(jnp.int32, (SUBLANES, BIAS_RING), 0)
    lane = lax.broadcasted_iota(jnp.int32, (SUBLANES, SEQ_BLOCK), 1)
    masked = jnp.full((SUBLANES, SEQ_BLOCK), NEG_INF, F32)
    for h in range(A_HEADS):
        base = ring_ref[h]
        shift = 1
        while shift < SUBLANES:
            base = jnp.where((sublane & shift) != 0, pltpu.roll(base, shift, 1), base)
            shift *= 2
        for a in range(n_keys // SUBLANES):
            key_chunk = a * SUBLANES // CHUNK
            rows = slice(a * SUBLANES, (a + 1) * SUBLANES)
            tile = pltpu.roll(base, a * SUBLANES, 1)[:, :SEQ_BLOCK]
            in_band = ((lane >= (key_chunk - N_LEFT_CHUNKS) * CHUNK)
                       & (lane < (key_chunk + 1) * CHUNK))
            tile = jnp.where(in_band, tile, NEG_INF)
            for v in range(KEY_BLOCKS):
                in_sequence = a * SUBLANES >= (KEY_BLOCKS - 1 - v) * SEQ_BLOCK
                bias_ref[v, h, rows, :] = tile if in_sequence else masked


def _attention_block(sb, q_ref, k_prev_ref, k_cur_ref, v_prev_ref, v_cur_ref, g_ref, bias_ref,
                     s_refs):
    lane = lax.broadcasted_iota(jnp.int32, (1, LANES), 1)
    low_half = lane < A_HEAD_DIM
    halves = (low_half, ~low_half)
    n_pairs = A_HEADS // HEADS_PER_LANE_GROUP
    rows = slice(sb * SEQ_BLOCK, (sb + 1) * SEQ_BLOCK)
    runtime_zero = jnp.minimum(pl.program_id(1), 0)
    variant = jnp.minimum(pl.program_id(1) * ATT_SUB + sb, KEY_BLOCKS - 1)

    def window(prev_ref, cur_ref, cols):
        blocks = []
        for back in range(KEY_BLOCKS - 1, -1, -1):
            blk = ATT_SUB + sb - back
            ref = prev_ref if blk < ATT_SUB else cur_ref
            start = (blk % ATT_SUB) * SEQ_BLOCK
            blocks.append(ref[start:start + SEQ_BLOCK, cols])
        return jnp.concatenate(blocks, axis=0)

    n_keys = KEY_BLOCKS * SEQ_BLOCK
    chunks_per_lane_group = LANES // CHUNK

    def band_rows(t):
        first_chunk = t * chunks_per_lane_group
        return slice(first_chunk * CHUNK,
                     (first_chunk + chunks_per_lane_group + N_LEFT_CHUNKS) * CHUNK)

    col_max = []
    for p in range(n_pairs):
        cols = slice(p * LANES, (p + 1) * LANES)
        q_pair = q_ref[rows, cols]
        k_pair = window(k_prev_ref, k_cur_ref, cols)
        for e in range(HEADS_PER_LANE_GROUP):
            h = p * HEADS_PER_LANE_GROUP + e
            q_h = jnp.where(halves[e], q_pair, jnp.zeros_like(q_pair))
            s = lax.dot_general(k_pair, q_h, (((1,), (1,)), ((), ())), preferred_element_type=F32)
            maxima = []
            for t in range(SEQ_BLOCK // LANES):
                band, lanes = band_rows(t), slice(t * LANES, (t + 1) * LANES)
                biased = s[band, lanes] + bias_ref[variant, h, band, lanes]
                s_refs[h][0, band, lanes] = biased
                maxima.append(jnp.max(biased, axis=0, keepdims=True))
            col_max.append(maxima)

    def second_pass():
        gated = []
        for p in range(n_pairs):
            cols = slice(p * LANES, (p + 1) * LANES)
            v_t = window(v_prev_ref, v_cur_ref, cols).T
            outs = []
            for e in range(HEADS_PER_LANE_GROUP):
                h = p * HEADS_PER_LANE_GROUP + e
                columns = []
                for t in range(SEQ_BLOCK // LANES):
                    band, lanes = band_rows(t), slice(t * LANES, (t + 1) * LANES)
                    in_band = jnp.exp2(s_refs[h][runtime_zero, band, lanes]
                                       - col_max[h][t]).astype(BF16)
                    pieces = [jnp.zeros((band.start, LANES), BF16)] if band.start else []
                    pieces.append(in_band)
                    if band.stop < n_keys:
                        pieces.append(jnp.zeros((n_keys - band.stop, LANES), BF16))
                    columns.append(jnp.concatenate(pieces, axis=0))
                probs = jnp.concatenate(columns, axis=1)
                own = slice(e * A_HEAD_DIM, (e + 1) * A_HEAD_DIM)
                v_h = jnp.concatenate([v_t[own], jnp.ones((DENOM_ROWS, n_keys), BF16)], axis=0)
                acc = jnp.dot(v_h, probs, preferred_element_type=F32)
                outs.append(acc[:A_HEAD_DIM] / acc[A_HEAD_DIM:A_HEAD_DIM + 1])
            o_pair = jnp.concatenate(outs, axis=0).T
            g = g_ref[rows, cols].astype(F32)
            gated.append((o_pair * _silu(g)).astype(BF16))
        return jnp.concatenate(gated, axis=1)

    return second_pass


def _attn_out_kernel(q_ref, k_prev_ref, k_cur_ref, v_prev_ref, v_cur_ref, g_ref, ring_ref,
                     ret_ref, ma_ref, mr_ref, x_ref, wa_ref, wr_ref, wo_ref, fg_ref, o_ref,
                     bias_ref, *s_refs):
    @pl.when((pl.program_id(0) == 0) & (pl.program_id(1) == 0))
    def _():
        _build_bias_tiles(ring_ref, bias_ref)

    second_passes = [
        _attention_block(sb, q_ref, k_prev_ref, k_cur_ref, v_prev_ref, v_cur_ref, g_ref, bias_ref,
                         s_refs[sb * A_HEADS:(sb + 1) * A_HEADS])
        for sb in range(ATT_SUB)]
    att = jnp.concatenate([second_pass() for second_pass in second_passes], axis=0)
    branch_a = jnp.dot(att, wa_ref[...], preferred_element_type=F32)
    branch_r = jnp.dot(ret_ref[...], wr_ref[...], preferred_element_type=F32)
    mixed = (_sigmoid(ma_ref[...].astype(F32)) * branch_a
             + _sigmoid(mr_ref[...].astype(F32)) * branch_r)
    y = x_ref[...] + jnp.dot(mixed.astype(BF16), wo_ref[...], preferred_element_type=F32)
    inv = lax.rsqrt(jnp.mean(y * y, axis=-1, keepdims=True) + NORM_EPS)
    o_ref[...] = y * inv * fg_ref[...]


def _attention_and_output(proj, ret, x2d, bias_ring, wa, wr, wo, final_gain, batch, seq_len):
    assert ATT_SUB >= KEY_BLOCKS - 1, "the key window must fit in the previous and current step"
    n_keys = KEY_BLOCKS * SEQ_BLOCK
    n_tok = proj.shape[0]
    step_rows = ATT_SUB * SEQ_BLOCK
    steps_per_seq = seq_len // step_rows

    def rows(b, j):
        return b * steps_per_seq + j

    def cur(col, width=COL_BLOCK):
        return pl.BlockSpec((step_rows, width), lambda b, j: (rows(b, j), col))

    def prev(col):
        return pl.BlockSpec((step_rows, COL_BLOCK),
                            lambda b, j: (rows(b, jnp.maximum(j - 1, 0)), col))

    return pl.pallas_call(
        _attn_out_kernel,
        grid=(batch, steps_per_seq),
        in_specs=[cur(COL_Q_A), prev(COL_K_A), cur(COL_K_A), prev(COL_V_A), cur(COL_V_A),
                  cur(COL_G_A), _resident(bias_ring.shape),
                  cur(0, R_V_WIDTH), cur(WIDE_M_A, D_MODEL), cur(WIDE_M_R, D_MODEL),
                  cur(0, D_MODEL),
                  _resident(wa.shape), _resident(wr.shape), _resident(wo.shape),
                  _resident(final_gain.shape)],
        out_specs=cur(0, D_MODEL),
        out_shape=jax.ShapeDtypeStruct((n_tok, D_MODEL), F32),
        scratch_shapes=[pltpu.VMEM((KEY_BLOCKS, A_HEADS, n_keys, SEQ_BLOCK), F32)]
        + [pltpu.VMEM((1, n_keys, SEQ_BLOCK), F32)] * (ATT_SUB * A_HEADS),
        compiler_params=pltpu.CompilerParams(
            dimension_semantics=("arbitrary", "arbitrary"), vmem_limit_bytes=VMEM_LIMIT_BYTES),
        name="attention_output",
    )(proj, proj, proj, proj, proj, proj, bias_ring, ret, proj, proj, x2d, wa, wr, wo, final_gain)


def _rotary_tables(seq_len):
    half = R_KEY_DIM // 2
    inv_freq = jnp.power(ROPE_BASE, -jnp.arange(half, dtype=F32) / half)
    ang = jnp.arange(seq_len).astype(F32)[:, None] * inv_freq[None, :]
    cos, sin = jnp.cos(ang), jnp.sin(ang)
    cos_t = jnp.tile(jnp.concatenate([cos, cos], axis=-1), (1, HEADS_PER_LANE_GROUP))
    sin_t = jnp.tile(jnp.concatenate([-sin, sin], axis=-1), (1, HEADS_PER_LANE_GROUP))
    return cos_t, sin_t


def _take_columns(table, idx):
    pieces, start = [], 0
    while start < len(idx):
        end = start + 1
        if end < len(idx) and idx[end] == idx[start]:
            while end < len(idx) and idx[end] == idx[start]:
                end += 1
            piece = jnp.broadcast_to(table[:, idx[start]:idx[start] + 1], (table.shape[0], end - start))
        else:
            while end < len(idx) and idx[end] == idx[end - 1] + 1:
                end += 1
            piece = table[:, idx[start]:idx[end - 1] + 1]
        pieces.append(piece)
        start = end
    return jnp.concatenate(pieces, axis=1)


def _bias_ring(rel_bias):
    n_keys = KEY_BLOCKS * SEQ_BLOCK
    assert n_keys + SEQ_BLOCK - 1 <= BIAS_RING
    u = np.arange(BIAS_RING)
    t = np.where(u < SEQ_BLOCK, u, u - BIAS_RING)
    idx = np.clip((KEY_BLOCKS - 1) * SEQ_BLOCK + t, -REL_CLIP, REL_CLIP) + REL_CLIP
    ring = _take_columns(rel_bias.astype(F32), idx) * LOG2_E
    return jnp.broadcast_to(ring[:, None, :], (rel_bias.shape[0], SUBLANES, BIAS_RING))


def _retention_tables():
    n_pairs = R_HEADS // HEADS_PER_LANE_GROUP
    log_gamma = jnp.log1p(-jnp.exp2(-5.0 - jnp.arange(R_HEADS, dtype=F32)))
    pos = jnp.arange(RET_BLOCK, dtype=F32)
    diff = pos[:, None] - pos[None, :]
    chunk_diff = (np.arange(RET_BLOCK)[:, None] // CHUNK) - (np.arange(RET_BLOCK)[None, :] // CHUNK)
    expo = jnp.where(chunk_diff == 0, jnp.abs(diff), diff)
    dmat = jnp.where((chunk_diff >= 0)[None], jnp.exp(expo[None] * log_gamma[:, None, None]), 0.0)
    dmat = dmat.reshape(n_pairs, HEADS_PER_LANE_GROUP, RET_BLOCK, RET_BLOCK)
    dmat = jnp.concatenate([dmat[:, e] for e in range(HEADS_PER_LANE_GROUP)], axis=-1)
    per_lane = lambda tab, width: jnp.tile(jnp.repeat(tab.T, width, axis=1), (SEQ_BLOCK // RET_BLOCK, 1))
    cross = per_lane(jnp.exp((pos + 1.0)[None, :] * log_gamma[:, None]), R_KEY_DIM)
    sdec = per_lane(jnp.exp((RET_BLOCK - 1.0 - pos)[None, :] * log_gamma[:, None]), R_KEY_DIM)
    block_decay = jnp.exp(RET_BLOCK * log_gamma)
    cdec = jnp.broadcast_to(jnp.repeat(block_decay, R_KEY_DIM).reshape(n_pairs, LANES, 1),
                            (n_pairs, LANES, HEADS_PER_LANE_GROUP * R_VAL_DIM))
    return dmat.astype(F32), cross, sdec, cdec


def kernel(x, norm_gain, w_in, rel_bias, gn_gain, w_out_attn, w_out_ret, w_out, final_gain):
    batch, seq_len, d_model = x.shape
    assert d_model == D_MODEL and norm_gain.shape[0] == 1, "single-layer block of width D_MODEL"
    assert seq_len % PROJ_ROWS == 0 and PROJ_ROWS % SEQ_BLOCK == 0
    assert seq_len % (ATT_SUB * SEQ_BLOCK) == 0
    x2d = x.reshape(batch * seq_len, D_MODEL)
    cos_t, sin_t = _rotary_tables(seq_len)
    proj, ret = _projection_and_retention(x2d, norm_gain[0][None, :], w_in[0].astype(BF16), cos_t, sin_t,
                                          _retention_tables(), gn_gain[0][None, :], seq_len)
    out = _attention_and_output(proj, ret, x2d, _bias_ring(rel_bias[0]), w_out_attn[0].astype(BF16),
                                w_out_ret[0].astype(BF16), w_out[0].astype(BF16),
                                final_gain[None, :], batch, seq_len)
    return out.reshape(batch, seq_len, D_MODEL)
```
